```python
import math
import jax, jax.numpy as jnp
from jax import lax
import numpy as np

D_MODEL = 1024
BATCH = 4
SEQ = 4096
DEPTH = 4
DEC_BATCH = 32
DEC_SEQ = 4
PAST_LEN = 8192
PAGE_SIZE = 128

N_MIXERS = 3
N_SGU_LAYERS = (DEPTH + 2) // 3
N_ATTN_LAYERS = (DEPTH + 1) // 3
N_SSM_LAYERS = DEPTH // 3

EPS = 1e-6
D_FF = 4 * D_MODEL
D_PLE = 256

SGU_CHUNK = 128
SGU_WIDTH = 2 * D_MODEL
SGU_GROUPS = 8
SGU_GDIM = SGU_WIDTH // SGU_GROUPS

ATTN_HEADS = D_MODEL // 128
ATTN_DK = D_MODEL // (2 * ATTN_HEADS)
ATTN_DV = 2 * ATTN_DK
ATTN_QKV_DIM = 2 * ATTN_HEADS * 2 * ATTN_DK + ATTN_HEADS * ATTN_DV
ROPE_THETA = 10000.0
Q_BLOCK = 128

SSM_INNER = 2 * D_MODEL
SSM_HEAD_DIM = 64
SSM_HEADS = SSM_INNER // SSM_HEAD_DIM
SSM_GROUPS = 4
SSM_STATE = 128
SSM_CONV = 4
SSM_CHUNK = 128
SSM_GN = SSM_GROUPS * SSM_STATE
SSM_CONV_DIM = SSM_INNER + 2 * SSM_GN
SSM_IN_DIM = SSM_INNER + SSM_CONV_DIM + SSM_HEADS

kernel_name = 'hybrid_sgu_diffattn_ssd_step'


def rmsnorm(x, g):
    xf = x.astype(jnp.float32)
    y = xf * lax.rsqrt(jnp.mean(xf * xf, axis=-1, keepdims=True) + EPS)
    return (y * g.astype(jnp.float32)).astype(x.dtype)


def layernorm(x, g, b):
    xf = x.astype(jnp.float32)
    mu = jnp.mean(xf, axis=-1, keepdims=True)
    xc = xf - mu
    y = xc * lax.rsqrt(jnp.mean(xc * xc, axis=-1, keepdims=True) + EPS)
    return (y * g.astype(jnp.float32) + b.astype(jnp.float32)).astype(x.dtype)


def rope(x, pos):
    half = x.shape[-1] // 2
    inv = ROPE_THETA ** (-jnp.arange(half, dtype=jnp.float32) / half)
    ang = pos.astype(jnp.float32)[:, None] * inv[None, :]
    cos = jnp.cos(ang)[None, :, None, None, :]
    sin = jnp.sin(ang)[None, :, None, None, :]
    xf = x.astype(jnp.float32)
    x1, x2 = xf[..., :half], xf[..., half:]
    return jnp.concatenate([x1 * cos - x2 * sin, x2 * cos + x1 * sin], axis=-1).astype(x.dtype)


def sgu_mixer(xn, w_in, ln_g, ln_b, w_s, b_s, w_o):
    b, L, _ = xn.shape
    uv = jax.nn.gelu(xn @ w_in)
    u, v = jnp.split(uv, 2, axis=-1)
    v = layernorm(v, ln_g, ln_b)
    q = min(SGU_CHUNK, L)
    nc = L // q
    tri = jnp.tril(jnp.ones((q, q), dtype=bool))
    ws = jnp.where(tri, w_s[:, :q, :q], 0)
    vc = v.reshape(b, nc, q, SGU_GROUPS, SGU_GDIM)
    s = jnp.einsum('gts,bcsge->bctge', ws, vc) + b_s[:, :q].T[None, None, :, :, None]
    out = (u * s.reshape(b, L, SGU_WIDTH)) @ w_o
    return out, v


def diff_attention(q, k, v, q_pos, k_pos, lam):
    b, Lq = q.shape[:2]
    bq = Q_BLOCK if Lq % Q_BLOCK == 0 else Lq
    nb = Lq // bq
    scale = ATTN_DK ** -0.5
    qb = jnp.moveaxis(q.reshape(b, nb, bq, ATTN_HEADS, 2, ATTN_DK), 1, 0)
    pb = q_pos.reshape(nb, bq)

    def block(args):
        qi, pi = args
        s = jnp.einsum('bqhjd,bkhjd->bhjqk', qi, k).astype(jnp.float32) * scale
        s = jnp.where(k_pos[None, :] <= pi[:, None], s, -jnp.inf)
        p = jax.nn.softmax(s, axis=-1)
        a = p[:, :, 0] - lam * p[:, :, 1]
        return jnp.einsum('bhqk,bkhe->bqhe', a.astype(v.dtype), v)

    o = lax.map(block, (qb, pb))
    return jnp.moveaxis(o, 0, 1).reshape(b, Lq, ATTN_HEADS, ATTN_DV)


def diff_attn_mixer(xn, q_pos, k_pos, past_k, past_v, w_qkv, lam_p, subln_g, w_o, lam_init):
    b, L, _ = xn.shape
    qk_dim = ATTN_HEADS * 2 * ATTN_DK
    qkv = xn @ w_qkv
    q = qkv[..., :qk_dim]
    k = qkv[..., qk_dim:2 * qk_dim]
    v = qkv[..., 2 * qk_dim:]
    q = rope(q.reshape(b, L, ATTN_HEADS, 2, ATTN_DK), q_pos)
    k = rope(k.reshape(b, L, ATTN_HEADS, 2, ATTN_DK), q_pos)
    v = v.reshape(b, L, ATTN_HEADS, ATTN_DV)
    if past_k is None:
        keys, vals = k, v
    else:
        keys = jnp.concatenate([past_k.astype(k.dtype), k], axis=1)
        vals = jnp.concatenate([past_v.astype(v.dtype), v], axis=1)
    lp = lam_p.astype(jnp.float32)
    lam = jnp.exp(jnp.sum(lp[0] * lp[1])) - jnp.exp(jnp.sum(lp[2] * lp[3])) + lam_init
    o = diff_attention(q, keys, vals, q_pos, k_pos, lam)
    o = rmsnorm(o, subln_g) * (1.0 - lam_init)
    out = o.reshape(b, L, ATTN_HEADS * ATTN_DV) @ w_o
    return out, k.reshape(b, L, ATTN_HEADS, 2 * ATTN_DK), v


def ssd_scan(x, dt, A, B, C, h0):
    f32 = jnp.float32
    b, L, H, P = x.shape
    G, N = B.shape[-2:]
    R = H // G
    q = SSM_CHUNK if L % SSM_CHUNK == 0 else L
    nc = L // q
    xd = (x.astype(f32) * dt[..., None]).reshape(b, nc, q, G, R, P)
    a = jnp.moveaxis((dt * A).reshape(b, nc, q, G, R), 2, -1)
    Bc = B.astype(f32).reshape(b, nc, q, G, N)
    Cc = C.astype(f32).reshape(b, nc, q, G, N)
    a_cum = jnp.cumsum(a, axis=-1)
    tri = jnp.tril(jnp.ones((q, q), dtype=bool))
    seg = a_cum[..., :, None] - a_cum[..., None, :]
    decay = jnp.exp(jnp.where(tri, seg, -jnp.inf))
    cb = jnp.einsum('bcqgn,bcsgn->bcgqs', Cc, Bc)
    y_diag = jnp.einsum('bcgrqs,bcsgrp->bcqgrp', cb[:, :, :, None] * decay, xd)
    w_state = jnp.exp(a_cum[..., -1:] - a_cum)
    chunk_states = jnp.einsum('bcqgn,bcgrq,bcqgrp->bcgrpn', Bc, w_state, xd)
    chunk_decay = jnp.exp(a_cum[..., -1])

    def step(h, inp):
        s_c, d_c = inp
        return h * d_c[..., None, None] + s_c, h

    hf, h_in = lax.scan(step, h0.astype(f32).reshape(b, G, R, P, N),
                        (jnp.moveaxis(chunk_states, 1, 0), jnp.moveaxis(chunk_decay, 1, 0)))
    y_off = jnp.einsum('bcqgn,cbgrpn,bcgrq->bcqgrp', Cc, h_in, jnp.exp(a_cum))
    y = (y_diag + y_off).reshape(b, L, H, P)
    return y, hf.reshape(b, H, P, N)


def ssm_mixer(xn, conv_buf, h0, w_in, conv_w, conv_b, dt_bias, a_log, d_skip, norm_g, w_o):
    b, L, _ = xn.shape
    zxbcdt = xn @ w_in
    z = zxbcdt[..., :SSM_INNER]
    xbc = zxbcdt[..., SSM_INNER:SSM_INNER + SSM_CONV_DIM]
    dt_raw = zxbcdt[..., SSM_INNER + SSM_CONV_DIM:]
    xpad = jnp.concatenate([conv_buf.astype(xbc.dtype), xbc], axis=1)
    conv = conv_b
    for tap in range(SSM_CONV):
        conv = conv + xpad[:, tap:tap + L] * conv_w[tap]
    conv = jax.nn.silu(conv)
    new_buf = xpad[:, L:]
    xs = conv[..., :SSM_INNER]
    Bm = conv[..., SSM_INNER:SSM_INNER + SSM_GN].reshape(b, L, SSM_GROUPS, SSM_STATE)
    Cm = conv[..., SSM_INNER + SSM_GN:].reshape(b, L, SSM_GROUPS, SSM_STATE)
    dt = jax.nn.softplus(dt_raw.astype(jnp.float32) + dt_bias.astype(jnp.float32))
    A = -jnp.exp(a_log.astype(jnp.float32))
    xh = xs.reshape(b, L, SSM_HEADS, SSM_HEAD_DIM)
    y, hf = ssd_scan(xh, dt, A, Bm, Cm, h0)
    y = y + xh.astype(jnp.float32) * d_skip.astype(jnp.float32)[:, None]
    y = y.reshape(b, L, SSM_INNER) * jax.nn.silu(z.astype(jnp.float32))
    y = rmsnorm(y.reshape(b, L, SSM_GROUPS, SSM_INNER // SSM_GROUPS),
                norm_g.reshape(SSM_GROUPS, SSM_INNER // SSM_GROUPS)).reshape(b, L, SSM_INNER)
    return y.astype(xn.dtype) @ w_o, new_buf, hf


def channel_and_ple(h, p, g_mlp, w_up, w_down, g_ple, w_gate, w_proj):
    xn = rmsnorm(h, g_mlp)
    h = h + jnp.square(jax.nn.relu(xn @ w_up)) @ w_down
    gate = jax.nn.sigmoid(rmsnorm(h, g_ple) @ w_gate)
    return h + gate * (p @ w_proj)


def setup_inputs(seed: int = 0) -> dict:
    key = jax.random.key(seed)
    ks = iter(jax.random.split(key, 64))
    f32 = jnp.float32

    def nrm(shape, scale):
        return jax.random.normal(next(ks), shape, f32) * scale

    def gain(shape):
        return 1.0 + nrm(shape, 0.05)

    n_pages = PAST_LEN // PAGE_SIZE
    n_pool = (DEC_BATCH * n_pages * 5) // 4
    d = {}
    d['x_prompt'] = nrm((BATCH, SEQ, D_MODEL), 1.0)
    d['x_sample'] = nrm((DEC_BATCH, DEC_SEQ, D_MODEL), 1.0)
    d['cache_k'] = nrm((N_ATTN_LAYERS, n_pool, PAGE_SIZE, ATTN_HEADS, 2 * ATTN_DK), 1.0)
    d['cache_v'] = nrm((N_ATTN_LAYERS, n_pool, PAGE_SIZE, ATTN_HEADS, ATTN_DV), 1.0)
    d['state_ssm'] = nrm((N_SSM_LAYERS, DEC_BATCH, SSM_HEADS, SSM_HEAD_DIM, SSM_STATE), 0.5)
    d['state_conv'] = nrm((N_SSM_LAYERS, DEC_BATCH, SSM_CONV - 1, SSM_CONV_DIM), 1.0)
    perm = jax.random.permutation(next(ks), n_pool)
    d['page_table'] = perm[:DEC_BATCH * n_pages].reshape(DEC_BATCH, n_pages).astype(jnp.int32)
    d['p_prompt'] = nrm((DEPTH, BATCH, SEQ, D_PLE), 1.0)
    d['p_sample'] = nrm((DEPTH, DEC_BATCH, DEC_SEQ, D_PLE), 1.0)
    d['norm_mix_g'] = gain((DEPTH, D_MODEL))
    d['norm_mlp_g'] = gain((DEPTH, D_MODEL))
    d['norm_ple_g'] = gain((DEPTH, D_MODEL))
    d['mlp_w_up'] = nrm((DEPTH, D_MODEL, D_FF), D_MODEL ** -0.5)
    d['mlp_w_down'] = nrm((DEPTH, D_FF, D_MODEL), 0.5 * D_FF ** -0.5)
    d['ple_w_gate'] = nrm((DEPTH, D_MODEL, D_MODEL), D_MODEL ** -0.5)
    d['ple_w_proj'] = nrm((DEPTH, D_PLE, D_MODEL), D_PLE ** -0.5)
    d['final_norm_g'] = gain((D_MODEL,))
    d['sgu_w_in'] = nrm((N_SGU_LAYERS, D_MODEL, 2 * SGU_WIDTH), D_MODEL ** -0.5)
    d['sgu_ln_g'] = gain((N_SGU_LAYERS, SGU_WIDTH))
    d['sgu_ln_b'] = nrm((N_SGU_LAYERS, SGU_WIDTH), 0.02)
    d['sgu_w_s'] = nrm((N_SGU_LAYERS, SGU_GROUPS, SGU_CHUNK, SGU_CHUNK), SGU_CHUNK ** -0.5)
    d['sgu_b_s'] = 1.0 + nrm((N_SGU_LAYERS, SGU_GROUPS, SGU_CHUNK), 0.1)
    d['sgu_w_out'] = nrm((N_SGU_LAYERS, SGU_WIDTH, D_MODEL), SGU_WIDTH ** -0.5)
    d['attn_w_qkv'] = nrm((N_ATTN_LAYERS, D_MODEL, ATTN_QKV_DIM), D_MODEL ** -0.5)
    d['attn_lambda'] = nrm((N_ATTN_LAYERS, 4, ATTN_DK), 0.1)
    d['attn_subln_g'] = gain((N_ATTN_LAYERS, ATTN_DV))
    d['attn_w_out'] = nrm((N_ATTN_LAYERS, ATTN_HEADS * ATTN_DV, D_MODEL), (ATTN_HEADS * ATTN_DV) ** -0.5)
    d['ssm_w_in'] = nrm((N_SSM_LAYERS, D_MODEL, SSM_IN_DIM), D_MODEL ** -0.5)
    d['ssm_conv_w'] = nrm((N_SSM_LAYERS, SSM_CONV, SSM_CONV_DIM), SSM_CONV ** -0.5)
    d['ssm_conv_b'] = nrm((N_SSM_LAYERS, SSM_CONV_DIM), 0.02)
    dt0 = jnp.exp(jax.random.uniform(next(ks), (N_SSM_LAYERS, SSM_HEADS), f32,
                                     math.log(1e-3), math.log(1e-1)))
    d['ssm_dt_bias'] = dt0 + jnp.log(-jnp.expm1(-dt0))
    d['ssm_a_log'] = jnp.log(jax.random.uniform(next(ks), (N_SSM_LAYERS, SSM_HEADS), f32, 1.0, 16.0))
    d['ssm_d'] = 1.0 + nrm((N_SSM_LAYERS, SSM_HEADS), 0.1)
    d['ssm_norm_g'] = gain((N_SSM_LAYERS, SSM_INNER))
    d['ssm_w_out'] = nrm((N_SSM_LAYERS, SSM_INNER, D_MODEL), SSM_INNER ** -0.5)
    return d


def reference(x_prompt, x_sample, cache_k, cache_v, state_ssm, state_conv, page_table, p_prompt, p_sample,
              norm_mix_g, norm_mlp_g, norm_ple_g, mlp_w_up, mlp_w_down, ple_w_gate, ple_w_proj, final_norm_g,
              sgu_w_in, sgu_ln_g, sgu_ln_b, sgu_w_s, sgu_b_s, sgu_w_out,
              attn_w_qkv, attn_lambda, attn_subln_g, attn_w_out,
              ssm_w_in, ssm_conv_w, ssm_conv_b, ssm_dt_bias, ssm_a_log, ssm_d, ssm_norm_g, ssm_w_out):
    b_p, l_p, _ = x_prompt.shape
    b_s, l_s, _ = x_sample.shape
    past_len = page_table.shape[1] * cache_k.shape[2]
    pos_p = jnp.arange(l_p, dtype=jnp.int32)
    pos_s = past_len + jnp.arange(l_s, dtype=jnp.int32)
    kpos_s = jnp.concatenate([jnp.arange(past_len, dtype=jnp.int32), pos_s])

    hp, hs = x_prompt, x_sample
    k_p, v_p, k_s, v_s = [], [], [], []
    ssm_p, conv_p, ssm_s, conv_s = [], [], [], []
    sgu_s = []
    for i in range(DEPTH):
        kind, j = i % N_MIXERS, i // N_MIXERS
        xp = rmsnorm(hp, norm_mix_g[i])
        xs = rmsnorm(hs, norm_mix_g[i])
        if kind == 0:
            mp, _ = sgu_mixer(xp, sgu_w_in[j], sgu_ln_g[j], sgu_ln_b[j], sgu_w_s[j], sgu_b_s[j], sgu_w_out[j])
            ms, v_rows = sgu_mixer(xs, sgu_w_in[j], sgu_ln_g[j], sgu_ln_b[j], sgu_w_s[j], sgu_b_s[j], sgu_w_out[j])
            sgu_s.append(v_rows)
        elif kind == 1:
            lam_init = 0.8 - 0.6 * math.exp(-0.3 * i)
            mp, kr, vr = diff_attn_mixer(xp, pos_p, pos_p, None, None,
                                         attn_w_qkv[j], attn_lambda[j], attn_subln_g[j], attn_w_out[j], lam_init)
            past_k = cache_k[j, page_table].reshape(b_s, past_len, ATTN_HEADS, 2, ATTN_DK)
            past_v = cache_v[j, page_table].reshape(b_s, past_len, ATTN_HEADS, ATTN_DV)
            ms, kr_s, vr_s = diff_attn_mixer(xs, pos_s, kpos_s, past_k, past_v,
                                             attn_w_qkv[j], attn_lambda[j], attn_subln_g[j], attn_w_out[j], lam_init)
            k_p.append(kr)
            v_p.append(vr)
            k_s.append(kr_s)
            v_s.append(vr_s)
        else:
            zero_buf = jnp.zeros((b_p, SSM_CONV - 1, SSM_CONV_DIM), xp.dtype)
            zero_h = jnp.zeros((b_p, SSM_HEADS, SSM_HEAD_DIM, SSM_STATE), jnp.float32)
            mp, cb_p, hf_p = ssm_mixer(xp, zero_buf, zero_h, ssm_w_in[j], ssm_conv_w[j], ssm_conv_b[j],
                                       ssm_dt_bias[j], ssm_a_log[j], ssm_d[j], ssm_norm_g[j], ssm_w_out[j])
            ms, cb_s, hf_s = ssm_mixer(xs, state_conv[j], state_ssm[j], ssm_w_in[j], ssm_conv_w[j], ssm_conv_b[j],
                                       ssm_dt_bias[j], ssm_a_log[j], ssm_d[j], ssm_norm_g[j], ssm_w_out[j])
            ssm_p.append(hf_p)
            conv_p.append(cb_p)
            ssm_s.append(hf_s)
            conv_s.append(cb_s)
        hp = channel_and_ple(hp + mp, p_prompt[i], norm_mlp_g[i], mlp_w_up[i], mlp_w_down[i],
                             norm_ple_g[i], ple_w_gate[i], ple_w_proj[i])
        hs = channel_and_ple(hs + ms, p_sample[i], norm_mlp_g[i], mlp_w_up[i], mlp_w_down[i],
                             norm_ple_g[i], ple_w_gate[i], ple_w_proj[i])

    y_prompt = rmsnorm(hp, final_norm_g)
    y_sample = rmsnorm(hs, final_norm_g)
    k_prompt = jnp.stack(k_p)
    v_prompt = jnp.stack(v_p)
    k_sample = jnp.stack(k_s)
    v_sample = jnp.stack(v_s)
    ssm_prompt = jnp.stack(ssm_p)
    conv_prompt = jnp.stack(conv_p)
    ssm_sample = jnp.stack(ssm_s)
    conv_sample = jnp.stack(conv_s)
    sgu_v_sample = jnp.stack(sgu_s)
    return (y_prompt, y_sample, k_prompt, v_prompt, k_sample, v_sample,
            ssm_prompt, conv_prompt, ssm_sample, conv_sample, sgu_v_sample)
```

```python
import functools
import math

import jax
import jax.numpy as jnp
from jax import lax
from jax.experimental import pallas as pl
from jax.experimental.pallas import tpu as pltpu

F32 = jnp.float32
BF16 = jnp.bfloat16

EPS = 1e-6
D_MODEL = 1024
D_FF = 4 * D_MODEL
D_PLE = 256
N_MIXERS = 3

SGU_CHUNK = 128
SGU_WIDTH = 2 * D_MODEL
SGU_GROUPS = 8
SGU_GDIM = SGU_WIDTH // SGU_GROUPS

ATTN_HEADS = 8
ATTN_DK = 64
ATTN_DV = 128
ATTN_HD = 2 * ATTN_DK
ROPE_THETA = 10000.0

SSM_INNER = 2 * D_MODEL
SSM_HEAD_DIM = 64
SSM_HEADS = SSM_INNER // SSM_HEAD_DIM
SSM_GROUPS = 4
SSM_STATE = 128
SSM_CONV = 4
SSM_CHUNK = 128
SSM_GN = SSM_GROUPS * SSM_STATE
SSM_CONV_DIM = SSM_INNER + 2 * SSM_GN
SSM_HPG = SSM_HEADS // SSM_GROUPS

VMEM_LIMIT_BYTES = 56 * 1024 * 1024
LANE = 128


def _cparams(*sem):
    return pltpu.CompilerParams(dimension_semantics=sem, vmem_limit_bytes=VMEM_LIMIT_BYTES)


def _const_spec(shape):
    zeros = (0,) * len(shape)
    return pl.BlockSpec(shape, lambda *_: zeros, pipeline_mode=pl.Buffered(1))


def _rms(x, g):
    ms = jnp.mean(x * x, axis=-1, keepdims=True)
    return (x * lax.rsqrt(ms + EPS)) * g


def _dot(a, b):
    return jnp.dot(a, b, preferred_element_type=F32)


def _dot_nt(a, b):
    return lax.dot_general(a, b, (((1,), (1,)), ((), ())), preferred_element_type=F32)


def _dot_tn(a, b):
    return lax.dot_general(a, b, (((0,), (0,)), ((), ())), preferred_element_type=F32)


def _row_tile(m, want):
    t = min(m, want)
    assert m % t == 0, (m, t)
    return t


def _channel_kernel(h_ref, mix_ref, p_ref, wmix_ref, gmlp_ref, wup_ref, wdown_ref, gple_ref,
                    wgate_ref, wproj_ref, gfin_ref, out_ref, *, final_norm):
    h = h_ref[...] + _dot(mix_ref[...], wmix_ref[...])
    xn = _rms(h, gmlp_ref[...]).astype(BF16)
    a = jnp.square(jnp.maximum(_dot(xn, wup_ref[...]), 0.0)).astype(BF16)
    h = h + _dot(a, wdown_ref[...])
    xn = _rms(h, gple_ref[...]).astype(BF16)
    gate = jax.nn.sigmoid(_dot(xn, wgate_ref[...]))
    h = h + gate * _dot(p_ref[...].astype(BF16), wproj_ref[...])
    if final_norm:
        h = _rms(h, gfin_ref[...])
    out_ref[...] = h


def channel_update(h, mix, p, w_mix, g_mlp, w_up, w_down, g_ple, w_gate, w_proj, g_fin, *,
                   final_norm, tile=512):
    m, d = h.shape
    dm = mix.shape[1]
    tm = _row_tile(m, tile)
    row = lambda w: pl.BlockSpec((tm, w), lambda i: (i, 0))
    return pl.pallas_call(
        functools.partial(_channel_kernel, final_norm=final_norm),
        grid=(m // tm,),
        in_specs=[row(d), row(dm), row(D_PLE), _const_spec((dm, d)), _const_spec((1, d)),
                  _const_spec((d, D_FF)), _const_spec((D_FF, d)), _const_spec((1, d)),
                  _const_spec((d, d)), _const_spec((D_PLE, d)), _const_spec((1, d))],
        out_specs=row(d),
        out_shape=jax.ShapeDtypeStruct((m, d), F32),
        compiler_params=_cparams("parallel"),
        name="channel_update",
    )(h, mix, p, w_mix, g_mlp.reshape(1, d), w_up, w_down, g_ple.reshape(1, d), w_gate, w_proj,
      g_fin.reshape(1, d))


def _sgu_kernel(h_ref, g_ref, win_ref, lng_ref, lnb_ref, ws_ref, bs_ref, mix_ref, *v_out,
                n_chunks):
    xn = _rms(h_ref[...], g_ref[...]).astype(BF16)
    uv = jax.nn.gelu(_dot(xn, win_ref[...]))
    u = uv[:, :SGU_WIDTH]
    v = uv[:, SGU_WIDTH:]
    mu = jnp.mean(v, axis=-1, keepdims=True)
    vc = v - mu
    v = (vc * lax.rsqrt(jnp.mean(vc * vc, axis=-1, keepdims=True) + EPS)) * lng_ref[...] + lnb_ref[...]
    if v_out:
        v_out[0][...] = v
    vb = v.astype(BF16)
    bs = bs_ref[...]
    for c in range(n_chunks):
        rows = slice(c * SGU_CHUNK, (c + 1) * SGU_CHUNK)
        for g in range(SGU_GROUPS):
            cols = slice(g * SGU_GDIM, (g + 1) * SGU_GDIM)
            s = _dot(ws_ref[g], vb[rows, cols]) + bs[:, g:g + 1]
            mix_ref[rows, cols] = (u[rows, cols] * s).astype(BF16)


def _sgu_spatial(w_s, b_s, seq):
    q = min(SGU_CHUNK, seq)
    tri = jnp.tril(jnp.ones((q, q), dtype=bool))
    ws = jnp.where(tri, w_s[:, :q, :q], 0)
    bs = b_s[:, :q]
    rep = SGU_CHUNK // q
    if rep > 1:
        eye = jnp.eye(rep, dtype=ws.dtype)
        ws = jnp.einsum("ab,gts->gatbs", eye, ws).reshape(SGU_GROUPS, SGU_CHUNK, SGU_CHUNK)
        bs = jnp.tile(bs, (1, rep))
    return ws.astype(BF16), bs.T


def sgu_mix(h, g_mix, w_in, ln_g, ln_b, ws, bs, *, want_v, tile=256):
    m, d = h.shape
    tm = _row_tile(m, tile)
    row = lambda w: pl.BlockSpec((tm, w), lambda i: (i, 0))
    out_shape = [jax.ShapeDtypeStruct((m, SGU_WIDTH), BF16)]
    out_specs = [row(SGU_WIDTH)]
    if want_v:
        out_shape.append(jax.ShapeDtypeStruct((m, SGU_WIDTH), F32))
        out_specs.append(row(SGU_WIDTH))
    outs = pl.pallas_call(
        functools.partial(_sgu_kernel, n_chunks=tm // SGU_CHUNK),
        grid=(m // tm,),
        in_specs=[row(d), _const_spec((1, d)), _const_spec((d, 2 * SGU_WIDTH)),
                  _const_spec((1, SGU_WIDTH)), _const_spec((1, SGU_WIDTH)),
                  _const_spec((SGU_GROUPS, SGU_CHUNK, SGU_CHUNK)),
                  _const_spec((SGU_CHUNK, SGU_GROUPS))],
        out_specs=out_specs,
        out_shape=out_shape,
        compiler_params=_cparams("parallel"),
        name="sgu_mix",
    )(h, g_mix.reshape(1, d), w_in, ln_g.reshape(1, -1), ln_b.reshape(1, -1), ws, bs)
    return outs if want_v else (outs[0], None)


def _rope_tables(pos):
    half = ATTN_DK // 2
    inv = ROPE_THETA ** (-jnp.arange(half, dtype=F32) / half)
    ang = pos.astype(F32)[:, None] * inv[None, :]
    cos, sin = jnp.cos(ang), jnp.sin(ang)
    cos_t = jnp.tile(cos, (1, 4))
    sin_t = jnp.tile(jnp.concatenate([-sin, sin], axis=1), (1, 2))
    return cos_t, sin_t


def _rope_head(x, cos, sin_signed, first_half):
    partner = jnp.where(first_half, pltpu.roll(x, LANE - ATTN_DK // 2, 1),
                        pltpu.roll(x, ATTN_DK // 2, 1))
    return x * cos + partner * sin_signed


def _qkv_kernel(h_ref, g_ref, w_ref, cos_ref, sin_ref, qb_ref, k_ref, v_ref, kb_ref, vb_ref):
    xn = _rms(h_ref[...], g_ref[...]).astype(BF16)
    qkv = _dot(xn, w_ref[...])
    cos = cos_ref[...]
    sin = sin_ref[...]
    lane = lax.broadcasted_iota(jnp.int32, cos.shape, 1)
    first_half = (lane % ATTN_DK) < (ATTN_DK // 2)
    qk_dim = ATTN_HEADS * ATTN_HD
    scale = ATTN_DK ** -0.5
    for hd in range(ATTN_HEADS):
        cols = slice(hd * ATTN_HD, (hd + 1) * ATTN_HD)
        q = _rope_head(qkv[:, cols], cos, sin, first_half)
        k = _rope_head(qkv[:, qk_dim + hd * ATTN_HD:qk_dim + (hd + 1) * ATTN_HD], cos, sin,
                       first_half)
        qb_ref[:, cols] = (q * scale).astype(BF16)
        k_ref[:, cols] = k
        kb_ref[:, cols] = k.astype(BF16)
    v = qkv[:, 2 * qk_dim:]
    v_ref[...] = v
    vb_ref[...] = v.astype(BF16)


def qkv_rope(h, g_mix, w_qkv, cos_t, sin_t, *, tile=512):
    m, d = h.shape
    seq = cos_t.shape[0]
    tm = _row_tile(m, tile)
    row = lambda w: pl.BlockSpec((tm, w), lambda i: (i, 0))
    if seq % tm == 0:
        n_blk = seq // tm
        table = pl.BlockSpec((tm, LANE), lambda i: (i % n_blk, 0))
    else:
        assert tm % seq == 0, (tm, seq)
        cos_t = jnp.tile(cos_t, (tm // seq, 1))
        sin_t = jnp.tile(sin_t, (tm // seq, 1))
        table = _const_spec((tm, LANE))
    hw = ATTN_HEADS * ATTN_HD
    return pl.pallas_call(
        _qkv_kernel,
        grid=(m // tm,),
        in_specs=[row(d), _const_spec((1, d)), _const_spec((d, 3 * hw)), table, table],
        out_specs=[row(hw)] * 5,
        out_shape=[jax.ShapeDtypeStruct((m, hw), BF16), jax.ShapeDtypeStruct((m, hw), F32),
                   jax.ShapeDtypeStruct((m, hw), F32), jax.ShapeDtypeStruct((m, hw), BF16),
                   jax.ShapeDtypeStruct((m, hw), BF16)],
        compiler_params=_cparams("parallel"),
        name="qkv_rope",
    )(h, g_mix.reshape(1, d), w_qkv, cos_t, sin_t)


def _lambda_value(lam_ref, lam_init):
    lp = lam_ref[...]
    s1 = jnp.sum(lp[0:1] * lp[1:2], axis=-1, keepdims=True)
    s2 = jnp.sum(lp[2:3] * lp[3:4], axis=-1, keepdims=True)
    return jnp.exp(s1) - jnp.exp(s2) + lam_init


def _softmax_step(s, v, m_ref, l_ref, acc_ref):
    m_prev = m_ref[...]
    m_new = jnp.maximum(m_prev, jnp.max(s, axis=-1, keepdims=True))
    alpha = jnp.exp(m_prev - m_new)
    p = jnp.exp(s - m_new)
    l_ref[...] = alpha * l_ref[...] + jnp.sum(p, axis=-1, keepdims=True)
    acc_ref[...] = alpha * acc_ref[...] + _dot(p.astype(BF16), v)
    m_ref[...] = m_new


def _attn_finish(lam_ref, g_ref, o_ref, l1, a1, l2, a2, lam_init):
    lam = _lambda_value(lam_ref, lam_init)
    o = a1[...] / l1[...] - lam * (a2[...] / l2[...])
    o = _rms(o, g_ref[...]) * (1.0 - lam_init)
    o_ref[...] = o.astype(BF16)


def _attn_prompt_kernel(lam_ref, g_ref, q_ref, k_ref, v_ref, o_ref, q1_ref, q2_ref,
                        m1, l1, a1, m2, l2, a2, *, tile, lam_init):
    qi = pl.program_id(2)
    ki = pl.program_id(3)

    @pl.when(ki == 0)
    def _():
        q = q_ref[...]
        lane = lax.broadcasted_iota(jnp.int32, q.shape, 1)
        zero = jnp.zeros_like(q)
        q1_ref[...] = jnp.where(lane < ATTN_DK, q, zero)
        q2_ref[...] = jnp.where(lane < ATTN_DK, zero, q)
        for m_ref, l_ref, a_ref in ((m1, l1, a1), (m2, l2, a2)):
            m_ref[...] = jnp.full(m_ref.shape, -jnp.inf, F32)
            l_ref[...] = jnp.zeros(l_ref.shape, F32)
            a_ref[...] = jnp.zeros(a_ref.shape, F32)

    def block(masked):
        k = k_ref[...]
        v = v_ref[...]
        s1 = _dot_nt(q1_ref[...], k)
        s2 = _dot_nt(q2_ref[...], k)
        if masked:
            r = lax.broadcasted_iota(jnp.int32, s1.shape, 0)
            c = lax.broadcasted_iota(jnp.int32, s1.shape, 1)
            s1 = jnp.where(c <= r, s1, -jnp.inf)
            s2 = jnp.where(c <= r, s2, -jnp.inf)
        _softmax_step(s1, v, m1, l1, a1)
        _softmax_step(s2, v, m2, l2, a2)

    @pl.when(ki < qi)
    def _():
        block(False)

    @pl.when(ki == qi)
    def _():
        block(True)
        _attn_finish(lam_ref, g_ref, o_ref, l1, a1, l2, a2, lam_init)


def attn_prompt(qb, kb, vb, lam_p, subln_g, *, batch, lam_init, tile=512):
    m, hw = qb.shape
    seq = m // batch
    t = _row_tile(seq, tile)
    n = seq // t
    q3, k3, v3 = (x.reshape(batch, seq, hw) for x in (qb, kb, vb))
    q_spec = pl.BlockSpec((None, t, ATTN_HD), lambda b, h, qi, ki: (b, qi, h))
    kv_spec = pl.BlockSpec((None, t, ATTN_HD), lambda b, h, qi, ki: (b, jnp.minimum(ki, qi), h))
    out = pl.pallas_call(
        functools.partial(_attn_prompt_kernel, tile=t, lam_init=lam_init),
        grid=(batch, ATTN_HEADS, n, n),
        in_specs=[_const_spec((4, ATTN_DK)), _const_spec((1, ATTN_DV)), q_spec, kv_spec, kv_spec],
        out_specs=q_spec,
        out_shape=jax.ShapeDtypeStruct((batch, seq, hw), BF16),
        scratch_shapes=[pltpu.VMEM((t, ATTN_HD), BF16), pltpu.VMEM((t, ATTN_HD), BF16),
                        pltpu.VMEM((t, 1), F32), pltpu.VMEM((t, 1), F32), pltpu.VMEM((t, ATTN_DV), F32),
                        pltpu.VMEM((t, 1), F32), pltpu.VMEM((t, 1), F32), pltpu.VMEM((t, ATTN_DV), F32)],
        compiler_params=_cparams("parallel", "parallel", "parallel", "arbitrary"),
        name="attn_prompt",
    )(lam_p, subln_g.reshape(1, ATTN_DV), q3, k3, v3)
    return out.reshape(m, hw)


DEC_COLS = ATTN_HEADS * 2
NEW_ROWS = 16


def _col_from_row(x_row, n):
    r = lax.broadcasted_iota(jnp.int32, (n, n), 0)
    c = lax.broadcasted_iota(jnp.int32, (n, n), 1)
    return jnp.sum(jnp.where(r == c, x_row, 0.0), axis=1, keepdims=True)


def _attn_decode_kernel(pt_ref, lam_ref, g_ref, wq_ref, kn_ref, vn_ref, *refs, pages, n_q, lam_init):
    k_refs = refs[:pages]
    v_refs = refs[pages:2 * pages]
    o_ref, m_ref, l_ref, acc_ref = refs[2 * pages:]
    step = pl.program_id(1)
    n_col = DEC_COLS * n_q

    @pl.when(step == 0)
    def _():
        m_ref[...] = jnp.full(m_ref.shape, -jnp.inf, F32)
        l_ref[...] = jnp.zeros(l_ref.shape, F32)
        acc_ref[...] = jnp.zeros(acc_ref.shape, F32)

    wq = wq_ref[...]

    def update(s_list, v_list):
        m_prev = m_ref[...]
        m_new = m_prev
        for s in s_list:
            m_new = jnp.maximum(m_new, jnp.max(s, axis=0, keepdims=True))
        alpha = jnp.exp(m_prev - m_new)
        l_new = alpha * l_ref[...]
        pv = None
        for s, v in zip(s_list, v_list):
            p = jnp.exp(s - m_new)
            l_new = l_new + jnp.sum(p, axis=0, keepdims=True)
            d = _dot_tn(p.astype(BF16), v)
            pv = d if pv is None else pv + d
        acc_ref[...] = _col_from_row(alpha, n_col) * acc_ref[...] + pv
        l_ref[...] = l_new
        m_ref[...] = m_new

    s_list, v_list = [], []
    for k_ref, v_ref in zip(k_refs, v_refs):
        s_list.append(_dot(k_ref[...].astype(BF16), wq))
        v_list.append(v_ref[...].astype(BF16))
    update(s_list, v_list)

    @pl.when(step == pl.num_programs(1) - 1)
    def _():
        s = _dot(kn_ref[...], wq)
        r = lax.broadcasted_iota(jnp.int32, s.shape, 0)
        c = lax.broadcasted_iota(jnp.int32, s.shape, 1)
        s = jnp.where(r <= c % n_q, s, -jnp.inf)
        update([s], [vn_ref[...]])
        lam = _lambda_value(lam_ref, lam_init)
        o_all = acc_ref[...] / _col_from_row(l_ref[...], n_col)
        for hd in range(ATTN_HEADS):
            cols = slice(hd * ATTN_DV, (hd + 1) * ATTN_DV)
            r1 = (2 * hd) * n_q
            o = o_all[r1:r1 + n_q, cols] - lam * o_all[r1 + n_q:r1 + 2 * n_q, cols]
            o = _rms(o, g_ref[...]) * (1.0 - lam_init)
            o_ref[:, cols] = o.astype(BF16)


def attn_decode(qb, kb_new, vb_new, cache_k, cache_v, page_table, layer, lam_p, subln_g, *,
                lam_init, pages=4):
    b, n_q, hw = qb.shape
    n_layers, n_pool, page = cache_k.shape[:3]
    n_pages = page_table.shape[1]
    assert n_pages % pages == 0 and n_q <= NEW_ROWS
    n_col = DEC_COLS * n_q
    ck = cache_k.reshape(n_layers * n_pool, page, hw)
    cv = cache_v.reshape(n_layers * n_pool, page, hw)
    pt = (page_table.astype(jnp.int32) + layer * n_pool).reshape(-1)
    q5 = qb.reshape(b, n_q, DEC_COLS, ATTN_DK)
    eye = jnp.eye(DEC_COLS, dtype=qb.dtype)
    wq = jnp.einsum("bqcd,ce->bcdeq", q5, eye).reshape(b, hw, n_col)
    pad = ((0, 0), (0, NEW_ROWS - n_q), (0, 0))
    kn = jnp.pad(kb_new, pad)
    vn = jnp.pad(vb_new, pad)

    def page_spec(i):
        return pl.BlockSpec((None, page, hw), lambda bi, s, pt_ref: (pt_ref[bi * n_pages + s * pages + i], 0, 0))

    per_b = lambda r, w: pl.BlockSpec((None, r, w), lambda bi, s, pt_ref: (bi, 0, 0))
    const = lambda shape: pl.BlockSpec(shape, lambda bi, s, pt_ref: (0,) * len(shape))
    grid_spec = pltpu.PrefetchScalarGridSpec(
        num_scalar_prefetch=1,
        grid=(b, n_pages // pages),
        in_specs=[const((4, ATTN_DK)), const((1, ATTN_DV)), per_b(hw, n_col), per_b(NEW_ROWS, hw),
                  per_b(NEW_ROWS, hw)] + [page_spec(i) for i in range(pages)] * 2,
        out_specs=per_b(n_q, hw),
        scratch_shapes=[pltpu.VMEM((1, n_col), F32), pltpu.VMEM((1, n_col), F32),
                        pltpu.VMEM((n_col, hw), F32)],
    )
    return pl.pallas_call(
        functools.partial(_attn_decode_kernel, pages=pages, n_q=n_q, lam_init=lam_init),
        grid_spec=grid_spec,
        out_shape=jax.ShapeDtypeStruct((b, n_q, hw), BF16),
        compiler_params=_cparams("parallel", "arbitrary"),
        name="attn_decode",
    )(pt, lam_p, subln_g.reshape(1, ATTN_DV), wq, kn, vn, *([ck] * pages), *([cv] * pages))


def _ssm_in_kernel(h_ref, g_ref, wz_ref, wx_ref, wdt_ref, z_ref, xbc_ref, dt_ref):
    xn = _rms(h_ref[...], g_ref[...]).astype(BF16)
    z_ref[...] = _dot(xn, wz_ref[...])
    xbc_ref[...] = _dot(xn, wx_ref[...])
    dt_ref[...] = _dot(xn, wdt_ref[...])


def ssm_in_proj(h, g_mix, w_z, w_xbc, w_dt, *, tile=512):
    m, d = h.shape
    tm = _row_tile(m, tile)
    row = lambda w: pl.BlockSpec((tm, w), lambda i: (i, 0))
    return pl.pallas_call(
        _ssm_in_kernel,
        grid=(m // tm,),
        in_specs=[row(d), _const_spec((1, d)), _const_spec((d, SSM_INNER)),
                  _const_spec((d, SSM_CONV_DIM)), _const_spec((d, LANE))],
        out_specs=[row(SSM_INNER), row(SSM_CONV_DIM), row(LANE)],
        out_shape=[jax.ShapeDtypeStruct((m, SSM_INNER), F32),
                   jax.ShapeDtypeStruct((m, SSM_CONV_DIM), F32),
                   jax.ShapeDtypeStruct((m, LANE), F32)],
        compiler_params=_cparams("parallel"),
        name="ssm_in_proj",
    )(h, g_mix.reshape(1, d), w_z, w_xbc, w_dt)


def _split3(x):
    x1 = x.astype(BF16)
    r = x - x1.astype(F32)
    x2 = r.astype(BF16)
    x3 = (r - x2.astype(F32)).astype(BF16)
    return x1, x2, x3


def _exact_dot(x, sel):
    x1, x2, x3 = _split3(x)
    return _dot(x1, sel) + _dot(x2, sel) + _dot(x3, sel)


def _exact_dot_left(sel, x):
    x1, x2, x3 = _split3(x)
    return _dot(sel, x1) + _dot(sel, x2) + _dot(sel, x3)


def _ssd_kernel(z_ref, xbc_ref, dt_ref, cbuf_ref, h0_ref, cw_ref, cb_ref, dtb_ref, alog_ref,
                dskip_ref, ng_ref, y_ref, hf_ref, state_ref, xpad_ref, *, valid):
    c = pl.program_id(1)
    q = SSM_CHUNK
    lead = 8

    @pl.when(c == 0)
    def _():
        state_ref[...] = h0_ref[...]
        xpad_ref[0:lead, :] = cbuf_ref[...]

    xpad_ref[lead:lead + q, :] = xbc_ref[...]
    conv = cb_ref[...]
    for tap in range(SSM_CONV):
        start = lead - (SSM_CONV - 1) + tap
        conv = conv + xpad_ref[start:start + q, :] * cw_ref[tap:tap + 1, :]
    xpad_ref[lead - (SSM_CONV - 1):lead, :] = xpad_ref[lead + q - (SSM_CONV - 1):lead + q, :]
    conv = conv * jax.nn.sigmoid(conv)
    xs = conv[:, :SSM_INNER]
    bm = conv[:, SSM_INNER:SSM_INNER + SSM_GN].astype(BF16)
    cm = conv[:, SSM_INNER + SSM_GN:].astype(BF16)

    row = lax.broadcasted_iota(jnp.int32, (q, q), 0)
    col = lax.broadcasted_iota(jnp.int32, (q, q), 1)
    tri = col <= row
    dt = jax.nn.softplus(dt_ref[...] + dtb_ref[...])
    if valid < q:
        dt = jnp.where(lax.broadcasted_iota(jnp.int32, dt.shape, 0) < valid, dt, 0.0)
    a = dt * (-jnp.exp(alog_ref[...]))
    acum = _exact_dot_left(tri.astype(BF16), a)
    acum_t = acum.T
    hr = lax.broadcasted_iota(jnp.int32, (LANE, SSM_INNER), 0)
    hc = lax.broadcasted_iota(jnp.int32, (LANE, SSM_INNER), 1)
    spread = (hc // SSM_HEAD_DIM == hr).astype(BF16)
    dt_x = _exact_dot(dt, spread)
    grow_x = _exact_dot(jnp.exp(acum), spread)
    wst_x = _exact_dot(jnp.exp(acum[q - 1:q, :] - acum), spread)
    chunk_decay = jnp.exp(acum_t[:, q - 1:q])

    xd = xs * dt_x
    xd_b = xd.astype(BF16)
    xw_b = (xd * wst_x).astype(BF16)
    y_parts = []
    for g in range(SSM_GROUPS):
        ncols = slice(g * SSM_STATE, (g + 1) * SSM_STATE)
        cb = _dot_nt(cm[:, ncols], bm[:, ncols])
        heads = range(g * SSM_HPG, (g + 1) * SSM_HPG)
        s_g = jnp.concatenate([state_ref[h] for h in heads], axis=0)
        gcols = slice(g * SSM_HPG * SSM_HEAD_DIM, (g + 1) * SSM_HPG * SSM_HEAD_DIM)
        y_off = _dot_nt(cm[:, ncols], s_g.astype(BF16)) * grow_x[:, gcols]
        y_diag = []
        for h in heads:
            seg = acum[:, h:h + 1] - acum_t[h:h + 1, :]
            lmat = (cb * jnp.exp(jnp.where(tri, seg, -jnp.inf))).astype(BF16)
            y_diag.append(_dot(lmat, xd_b[:, h * SSM_HEAD_DIM:(h + 1) * SSM_HEAD_DIM]))
        y_parts.append(jnp.concatenate(y_diag, axis=1) + y_off)
        new_states = _dot_tn(xw_b[:, gcols], bm[:, ncols])
        for i, h in enumerate(heads):
            state_ref[h] = state_ref[h] * chunk_decay[h:h + 1, :] + new_states[i * SSM_HEAD_DIM:(i + 1) * SSM_HEAD_DIM, :]
    y = jnp.concatenate(y_parts, axis=1) + xs * dskip_ref[...]
    zf = z_ref[...]
    y = y * (zf * jax.nn.sigmoid(zf))
    gw = SSM_INNER // SSM_GROUPS
    out = []
    for g in range(SSM_GROUPS):
        yg = y[:, g * gw:(g + 1) * gw]
        out.append(_rms(yg, ng_ref[:, g * gw:(g + 1) * gw]))
    y_ref[...] = jnp.concatenate(out, axis=1).astype(BF16)

    @pl.when(c == pl.num_programs(1) - 1)
    def _():
        hf_ref[...] = state_ref[...]


def ssd_scan(z, xbc, dt_raw, conv_buf, h0, conv_w, conv_b, dt_bias, a_log, d_skip, norm_g, *,
             batch, valid):
    m = z.shape[0]
    seq = m // batch
    q = SSM_CHUNK
    assert seq % q == 0 and (valid == q or seq == q)
    nc = seq // q
    z3 = z.reshape(batch, seq, SSM_INNER)
    x3 = xbc.reshape(batch, seq, SSM_CONV_DIM)
    d3 = dt_raw.reshape(batch, seq, LANE)
    cbuf = jnp.pad(conv_buf, ((0, 0), (8 - (SSM_CONV - 1), 0), (0, 0)))
    lane_pad = lambda v: jnp.pad(v.reshape(1, -1), ((0, 0), (0, LANE - v.shape[-1])))
    chunk = lambda w: pl.BlockSpec((None, q, w), lambda b, c: (b, c, 0))
    state_spec = pl.BlockSpec((None, SSM_HEADS, SSM_HEAD_DIM, SSM_STATE), lambda b, c: (b, 0, 0, 0))
    const = lambda shape: pl.BlockSpec(shape, lambda b, c: (0,) * len(shape))
    y, hf = pl.pallas_call(
        functools.partial(_ssd_kernel, valid=valid),
        grid=(batch, nc),
        in_specs=[chunk(SSM_INNER), chunk(SSM_CONV_DIM), chunk(LANE),
                  pl.BlockSpec((None, 8, SSM_CONV_DIM), lambda b, c: (b, 0, 0)), state_spec,
                  const((SSM_CONV, SSM_CONV_DIM)), const((1, SSM_CONV_DIM)), const((1, LANE)),
                  const((1, LANE)), const((1, SSM_INNER)), const((1, SSM_INNER))],
        out_specs=[chunk(SSM_INNER), state_spec],
        out_shape=[jax.ShapeDtypeStruct((batch, seq, SSM_INNER), BF16),
                   jax.ShapeDtypeStruct(h0.shape, F32)],
        scratch_shapes=[pltpu.VMEM((SSM_HEADS, SSM_HEAD_DIM, SSM_STATE), F32),
                        pltpu.VMEM((8 + q, SSM_CONV_DIM), F32)],
        compiler_params=_cparams("parallel", "arbitrary"),
        name="ssd_scan",
    )(z3, x3, d3, cbuf, h0, conv_w, conv_b.reshape(1, -1), lane_pad(dt_bias), lane_pad(a_log),
      jnp.repeat(d_skip, SSM_HEAD_DIM).reshape(1, -1), norm_g.reshape(1, -1))
    return y.reshape(m, SSM_INNER), hf


def _ssm_group(h, batch, seq, conv_buf, h0, g_mix, w_z, w_xbc, w_dt, conv_w, conv_b, dt_bias, a_log,
               d_skip, norm_g):
    d = h.shape[1]
    if seq % SSM_CHUNK == 0:
        padded, valid = seq, SSM_CHUNK
        hin = h
    else:
        assert seq < SSM_CHUNK
        padded, valid = SSM_CHUNK, seq
        hin = jnp.pad(h.reshape(batch, seq, d), ((0, 0), (0, padded - seq), (0, 0))).reshape(-1, d)
    z, xbc, dt = ssm_in_proj(hin, g_mix, w_z, w_xbc, w_dt)
    y, hf = ssd_scan(z, xbc, dt, conv_buf, h0, conv_w, conv_b, dt_bias, a_log, d_skip, norm_g,
                     batch=batch, valid=valid)
    xbc3 = xbc.reshape(batch, padded, SSM_CONV_DIM)[:, :seq]
    new_buf = jnp.concatenate([conv_buf, xbc3], axis=1)[:, seq:]
    y = y.reshape(batch, padded, SSM_INNER)[:, :seq].reshape(batch * seq, SSM_INNER)
    return y, new_buf, hf


def kernel(x_prompt, x_sample, cache_k, cache_v, state_ssm, state_conv, page_table, p_prompt, p_sample, norm_mix_g, norm_mlp_g, norm_ple_g, mlp_w_up, mlp_w_down, ple_w_gate, ple_w_proj, final_norm_g, sgu_w_in, sgu_ln_g, sgu_ln_b, sgu_w_s, sgu_b_s, sgu_w_out, attn_w_qkv, attn_lambda, attn_subln_g, attn_w_out, ssm_w_in, ssm_conv_w, ssm_conv_b, ssm_dt_bias, ssm_a_log, ssm_d, ssm_norm_g, ssm_w_out):
    b_p, l_p, d = x_prompt.shape
    b_s, l_s, _ = x_sample.shape
    depth = norm_mix_g.shape[0]
    past_len = page_table.shape[1] * cache_k.shape[2]
    m_s = b_s * l_s
    assert m_s % SGU_CHUNK == 0 and SGU_CHUNK % l_s == 0 and l_p % SGU_CHUNK == 0
    bf = lambda w: w.astype(BF16)

    hp = x_prompt.reshape(b_p * l_p, d)
    hs = x_sample.reshape(m_s, d)
    rope_p = _rope_tables(jnp.arange(l_p, dtype=jnp.int32))
    rope_s = _rope_tables(past_len + jnp.arange(l_s, dtype=jnp.int32))

    k_p, v_p, k_s, v_s = [], [], [], []
    ssm_p, conv_p, ssm_s, conv_s = [], [], [], []
    sgu_s = []
    for i in range(depth):
        kind, j = i % N_MIXERS, i // N_MIXERS
        g_mix = norm_mix_g[i]
        if kind == 0:
            w_in = bf(sgu_w_in[j])
            ws_p, bs_p = _sgu_spatial(sgu_w_s[j], sgu_b_s[j], l_p)
            ws_s, bs_s = _sgu_spatial(sgu_w_s[j], sgu_b_s[j], l_s)
            mix_p, _ = sgu_mix(hp, g_mix, w_in, sgu_ln_g[j], sgu_ln_b[j], ws_p, bs_p, want_v=False)
            mix_s, v_rows = sgu_mix(hs, g_mix, w_in, sgu_ln_g[j], sgu_ln_b[j], ws_s, bs_s,
                                    want_v=True, tile=SGU_CHUNK)
            sgu_s.append(v_rows.reshape(b_s, l_s, SGU_WIDTH))
            w_mix = bf(sgu_w_out[j])
        elif kind == 1:
            lam_init = 0.8 - 0.6 * math.exp(-0.3 * i)
            w_qkv = bf(attn_w_qkv[j])
            qb, k, v, kb, vb = qkv_rope(hp, g_mix, w_qkv, *rope_p)
            mix_p = attn_prompt(qb, kb, vb, attn_lambda[j], attn_subln_g[j], batch=b_p, lam_init=lam_init)
            k_p.append(k.reshape(b_p, l_p, ATTN_HEADS, ATTN_HD))
            v_p.append(v.reshape(b_p, l_p, ATTN_HEADS, ATTN_DV))
            qb, k, v, kb, vb = qkv_rope(hs, g_mix, w_qkv, *rope_s, tile=SGU_CHUNK)
            seq3 = lambda x: x.reshape(b_s, l_s, -1)
            mix_s = attn_decode(seq3(qb), seq3(kb), seq3(vb), cache_k, cache_v, page_table, j,
                                attn_lambda[j], attn_subln_g[j], lam_init=lam_init).reshape(m_s, -1)
            k_s.append(k.reshape(b_s, l_s, ATTN_HEADS, ATTN_HD))
            v_s.append(v.reshape(b_s, l_s, ATTN_HEADS, ATTN_DV))
            w_mix = bf(attn_w_out[j])
        else:
            w_in = ssm_w_in[j]
            w_z = bf(w_in[:, :SSM_INNER])
            w_xbc = bf(w_in[:, SSM_INNER:SSM_INNER + SSM_CONV_DIM])
            w_dt = bf(jnp.pad(w_in[:, SSM_INNER + SSM_CONV_DIM:], ((0, 0), (0, LANE - SSM_HEADS))))
            shared = (g_mix, w_z, w_xbc, w_dt, ssm_conv_w[j], ssm_conv_b[j], ssm_dt_bias[j],
                      ssm_a_log[j], ssm_d[j], ssm_norm_g[j])
            zero_buf = jnp.zeros((b_p, SSM_CONV - 1, SSM_CONV_DIM), F32)
            zero_h = jnp.zeros((b_p, SSM_HEADS, SSM_HEAD_DIM, SSM_STATE), F32)
            mix_p, cb_p, hf_p = _ssm_group(hp, b_p, l_p, zero_buf, zero_h, *shared)
            mix_s, cb_s, hf_s = _ssm_group(hs, b_s, l_s, state_conv[j], state_ssm[j], *shared)
            ssm_p.append(hf_p)
            conv_p.append(cb_p)
            ssm_s.append(hf_s)
            conv_s.append(cb_s)
            w_mix = bf(ssm_w_out[j])
        last = i == depth - 1
        weights = (w_mix, norm_mlp_g[i], bf(mlp_w_up[i]), bf(mlp_w_down[i]), norm_ple_g[i],
                   bf(ple_w_gate[i]), bf(ple_w_proj[i]), final_norm_g)
        hp = channel_update(hp, mix_p, p_prompt[i].reshape(b_p * l_p, D_PLE), *weights, final_norm=last)
        hs = channel_update(hs, mix_s, p_sample[i].reshape(m_s, D_PLE), *weights, final_norm=last,
                            tile=SGU_CHUNK)

    return (hp.reshape(b_p, l_p, d), hs.reshape(b_s, l_s, d),
            jnp.stack(k_p), jnp.stack(v_p), jnp.stack(k_s), jnp.stack(v_s),
            jnp.stack(ssm_p), jnp.stack(conv_p), jnp.stack(ssm_s), jnp.stack(conv_s),
            jnp.stack(sgu_s))
```

```python
import functools
import math

import jax
import jax.numpy as jnp
from jax import lax
from jax.experimental import pallas as pl
from jax.experimental.pallas import tpu as pltpu

F32 = jnp.float32
BF16 = jnp.bfloat16

EPS = 1e-6
D_MODEL = 1024
D_FF = 4 * D_MODEL
D_PLE = 256
N_MIXERS = 3

SGU_CHUNK = 128
SGU_WIDTH = 2 * D_MODEL
SGU_GROUPS = 8
SGU_GDIM = SGU_WIDTH // SGU_GROUPS

ATTN_HEADS = 8
ATTN_DK = 64
ATTN_DV = 128
ATTN_HD = 2 * ATTN_DK
ROPE_THETA = 10000.0

SSM_INNER = 2 * D_MODEL
SSM_HEAD_DIM = 64
SSM_HEADS = SSM_INNER // SSM_HEAD_DIM
SSM_GROUPS = 4
SSM_STATE = 128
SSM_CONV = 4
SSM_CHUNK = 128
SSM_SHORT_CHUNK = 16
SSM_GN = SSM_GROUPS * SSM_STATE
SSM_CONV_DIM = SSM_INNER + 2 * SSM_GN
SSM_HPG = SSM_HEADS // SSM_GROUPS

VMEM_LIMIT_BYTES = 56 * 1024 * 1024
LANE = 128


def _cparams(*sem):
    return pltpu.CompilerParams(dimension_semantics=sem, vmem_limit_bytes=VMEM_LIMIT_BYTES)


def _const_spec(shape):
    zeros = (0,) * len(shape)
    return pl.BlockSpec(shape, lambda *_: zeros, pipeline_mode=pl.Buffered(1))


def _rms(x, g):
    ms = jnp.mean(x * x, axis=-1, keepdims=True)
    return (x * lax.rsqrt(ms + EPS)) * g


def _dot(a, b):
    return jnp.dot(a, b, preferred_element_type=F32)


def _dot_nt(a, b):
    return lax.dot_general(a, b, (((1,), (1,)), ((), ())), preferred_element_type=F32)


def _dot_tn(a, b):
    return lax.dot_general(a, b, (((0,), (0,)), ((), ())), preferred_element_type=F32)


def _row_tile(m, want):
    t = min(m, want)
    assert m % t == 0, (m, t)
    return t


def _channel_kernel(h_ref, mix_ref, p_ref, wmix_ref, gmlp_ref, wup_ref, wdown_ref, gple_ref,
                    wgate_ref, wproj_ref, gfin_ref, out_ref, *, final_norm):
    h = h_ref[...] + _dot(mix_ref[...], wmix_ref[...])
    xn = _rms(h, gmlp_ref[...]).astype(BF16)
    a = jnp.square(jnp.maximum(_dot(xn, wup_ref[...]), 0.0)).astype(BF16)
    h = h + _dot(a, wdown_ref[...])
    xn = _rms(h, gple_ref[...]).astype(BF16)
    gate = jax.nn.sigmoid(_dot(xn, wgate_ref[...]))
    h = h + gate * _dot(p_ref[...].astype(BF16), wproj_ref[...])
    if final_norm:
        h = _rms(h, gfin_ref[...])
    out_ref[...] = h


def channel_update(h, mix, p, w_mix, g_mlp, w_up, w_down, g_ple, w_gate, w_proj, g_fin, *,
                   final_norm, tile=512):
    m, d = h.shape
    dm = mix.shape[1]
    tm = _row_tile(m, tile)
    row = lambda w: pl.BlockSpec((tm, w), lambda i: (i, 0))
    return pl.pallas_call(
        functools.partial(_channel_kernel, final_norm=final_norm),
        grid=(m // tm,),
        in_specs=[row(d), row(dm), row(D_PLE), _const_spec((dm, d)), _const_spec((1, d)),
                  _const_spec((d, D_FF)), _const_spec((D_FF, d)), _const_spec((1, d)),
                  _const_spec((d, d)), _const_spec((D_PLE, d)), _const_spec((1, d))],
        out_specs=row(d),
        out_shape=jax.ShapeDtypeStruct((m, d), F32),
        compiler_params=_cparams("parallel"),
        name="channel_update",
    )(h, mix, p, w_mix, g_mlp.reshape(1, d), w_up, w_down, g_ple.reshape(1, d), w_gate, w_proj,
      g_fin.reshape(1, d))


def _sgu_kernel(h_ref, g_ref, win_ref, lng_ref, lnb_ref, ws_ref, bs_ref, mix_ref, *v_out,
                n_chunks):
    xn = _rms(h_ref[...], g_ref[...]).astype(BF16)
    uv = jax.nn.gelu(_dot(xn, win_ref[...]))
    u = uv[:, :SGU_WIDTH]
    v = uv[:, SGU_WIDTH:]
    mu = jnp.mean(v, axis=-1, keepdims=True)
    vc = v - mu
    v = (vc * lax.rsqrt(jnp.mean(vc * vc, axis=-1, keepdims=True) + EPS)) * lng_ref[...] + lnb_ref[...]
    if v_out:
        v_out[0][...] = v
    vb = v.astype(BF16)
    bs = bs_ref[...]
    for c in range(n_chunks):
        rows = slice(c * SGU_CHUNK, (c + 1) * SGU_CHUNK)
        for g in range(SGU_GROUPS):
            cols = slice(g * SGU_GDIM, (g + 1) * SGU_GDIM)
            s = _dot(ws_ref[g], vb[rows, cols]) + bs[:, g:g + 1]
            mix_ref[rows, cols] = (u[rows, cols] * s).astype(BF16)


def _sgu_spatial(w_s, b_s, seq):
    q = min(SGU_CHUNK, seq)
    tri = jnp.tril(jnp.ones((q, q), dtype=bool))
    ws = jnp.where(tri, w_s[:, :q, :q], 0)
    bs = b_s[:, :q]
    rep = SGU_CHUNK // q
    if rep > 1:
        eye = jnp.eye(rep, dtype=ws.dtype)
        ws = jnp.einsum("ab,gts->gatbs", eye, ws).reshape(SGU_GROUPS, SGU_CHUNK, SGU_CHUNK)
        bs = jnp.tile(bs, (1, rep))
    return ws.astype(BF16), bs.T


def sgu_mix(h, g_mix, w_in, ln_g, ln_b, ws, bs, *, want_v, tile=256):
    m, d = h.shape
    tm = _row_tile(m, tile)
    row = lambda w: pl.BlockSpec((tm, w), lambda i: (i, 0))
    out_shape = [jax.ShapeDtypeStruct((m, SGU_WIDTH), BF16)]
    out_specs = [row(SGU_WIDTH)]
    if want_v:
        out_shape.append(jax.ShapeDtypeStruct((m, SGU_WIDTH), F32))
        out_specs.append(row(SGU_WIDTH))
    outs = pl.pallas_call(
        functools.partial(_sgu_kernel, n_chunks=tm // SGU_CHUNK),
        grid=(m // tm,),
        in_specs=[row(d), _const_spec((1, d)), _const_spec((d, 2 * SGU_WIDTH)),
                  _const_spec((1, SGU_WIDTH)), _const_spec((1, SGU_WIDTH)),
                  _const_spec((SGU_GROUPS, SGU_CHUNK, SGU_CHUNK)),
                  _const_spec((SGU_CHUNK, SGU_GROUPS))],
        out_specs=out_specs,
        out_shape=out_shape,
        compiler_params=_cparams("parallel"),
        name="sgu_mix",
    )(h, g_mix.reshape(1, d), w_in, ln_g.reshape(1, -1), ln_b.reshape(1, -1), ws, bs)
    return outs if want_v else (outs[0], None)


def _rope_tables(pos):
    half = ATTN_DK // 2
    inv = ROPE_THETA ** (-jnp.arange(half, dtype=F32) / half)
    ang = pos.astype(F32)[:, None] * inv[None, :]
    cos, sin = jnp.cos(ang), jnp.sin(ang)
    cos_t = jnp.tile(cos, (1, 4))
    sin_t = jnp.tile(jnp.concatenate([-sin, sin], axis=1), (1, 2))
    return cos_t, sin_t


def _rope_head(x, cos, sin_signed, first_half):
    partner = jnp.where(first_half, pltpu.roll(x, LANE - ATTN_DK // 2, 1),
                        pltpu.roll(x, ATTN_DK // 2, 1))
    return x * cos + partner * sin_signed


def _qkv_kernel(h_ref, g_ref, w_ref, cos_ref, sin_ref, qb_ref, k_ref, v_ref, kb_ref, vb_ref, *,
                transposed_v):
    xn = _rms(h_ref[...], g_ref[...]).astype(BF16)
    qkv = _dot(xn, w_ref[...])
    cos = cos_ref[...]
    sin = sin_ref[...]
    lane = lax.broadcasted_iota(jnp.int32, cos.shape, 1)
    first_half = (lane % ATTN_DK) < (ATTN_DK // 2)
    qk_dim = ATTN_HEADS * ATTN_HD
    scale = ATTN_DK ** -0.5
    rows = h_ref.shape[0]
    for hd in range(ATTN_HEADS):
        cols = slice(hd * ATTN_HD, (hd + 1) * ATTN_HD)
        q = _rope_head(qkv[:, cols], cos, sin, first_half)
        k = _rope_head(qkv[:, qk_dim + hd * ATTN_HD:qk_dim + (hd + 1) * ATTN_HD], cos, sin,
                       first_half)
        qb_ref[:, cols] = (q * scale).astype(BF16)
        k_ref[pl.ds(hd, rows, stride=ATTN_HEADS), :] = k
        kb_ref[:, cols] = k.astype(BF16)
        v_ref[pl.ds(hd, rows, stride=ATTN_HEADS), :] = qkv[:, 2 * qk_dim + hd * ATTN_DV:
                                                           2 * qk_dim + (hd + 1) * ATTN_DV]
    v = qkv[:, 2 * qk_dim:]
    vb_ref[...] = (v.T if transposed_v else v).astype(BF16)


def qkv_rope(h, g_mix, w_qkv, cos_t, sin_t, *, transposed_v=False, tile=512):
    m, d = h.shape
    seq = cos_t.shape[0]
    tm = _row_tile(m, tile)
    row = lambda w: pl.BlockSpec((tm, w), lambda i: (i, 0))
    if seq % tm == 0:
        n_blk = seq // tm
        table = pl.BlockSpec((tm, LANE), lambda i: (i % n_blk, 0))
    else:
        assert tm % seq == 0, (tm, seq)
        cos_t = jnp.tile(cos_t, (tm // seq, 1))
        sin_t = jnp.tile(sin_t, (tm // seq, 1))
        table = _const_spec((tm, LANE))
    hw = ATTN_HEADS * ATTN_HD
    vb_spec, vb_shape = row(hw), jax.ShapeDtypeStruct((m, hw), BF16)
    head_rows = pl.BlockSpec((tm * ATTN_HEADS, ATTN_HD), lambda i: (i, 0))
    if transposed_v:
        assert seq % tm == 0
        n_blk = seq // tm
        vb_spec = pl.BlockSpec((None, hw, tm), lambda i: (i // n_blk, 0, i % n_blk))
        vb_shape = jax.ShapeDtypeStruct((m // seq, hw, seq), BF16)
    return pl.pallas_call(
        functools.partial(_qkv_kernel, transposed_v=transposed_v),
        grid=(m // tm,),
        in_specs=[row(d), _const_spec((1, d)), _const_spec((d, 3 * hw)), table, table],
        out_specs=[row(hw), head_rows, head_rows, row(hw), vb_spec],
        out_shape=[jax.ShapeDtypeStruct((m, hw), BF16),
                   jax.ShapeDtypeStruct((m * ATTN_HEADS, ATTN_HD), F32),
                   jax.ShapeDtypeStruct((m * ATTN_HEADS, ATTN_DV), F32),
                   jax.ShapeDtypeStruct((m, hw), BF16), vb_shape],
        compiler_params=_cparams("parallel"),
        name="qkv_rope",
    )(h, g_mix.reshape(1, d), w_qkv, cos_t, sin_t)


def _lambda_value(lam_ref, lam_init):
    lp = lam_ref[...]
    s1 = jnp.sum(lp[0:1] * lp[1:2], axis=-1, keepdims=True)
    s2 = jnp.sum(lp[2:3] * lp[3:4], axis=-1, keepdims=True)
    return jnp.exp(s1) - jnp.exp(s2) + lam_init


def _softmax_step_t(s, v, m_prev, l_prev, acc_prev):
    m_new = jnp.maximum(m_prev, jnp.max(s, axis=0, keepdims=True))
    alpha = jnp.exp(m_prev - m_new)
    p = jnp.exp(s - m_new)
    l_new = alpha * l_prev + jnp.sum(p, axis=0, keepdims=True)
    acc = alpha * acc_prev + _dot(v, p.astype(BF16))
    return m_new, l_new, acc


def _attn_prompt_kernel(lam_ref, g_ref, q_ref, k_ref, v_ref, o_ref, *, tile, lam_init):
    qi = pl.program_id(2)
    q = q_ref[...]
    lane = lax.broadcasted_iota(jnp.int32, q.shape, 1)
    zero = jnp.zeros_like(q)
    q1 = jnp.where(lane < ATTN_DK, q, zero)
    q2 = jnp.where(lane < ATTN_DK, zero, q)

    def block(ki, carry, masked):
        start = pl.multiple_of(ki * tile, tile)
        k = k_ref[pl.ds(start, tile), :]
        v = v_ref[:, pl.ds(start, tile)]
        s1 = _dot_nt(k, q1)
        s2 = _dot_nt(k, q2)
        if masked:
            kr = lax.broadcasted_iota(jnp.int32, s1.shape, 0)
            qc = lax.broadcasted_iota(jnp.int32, s1.shape, 1)
            s1 = jnp.where(kr <= qc, s1, -jnp.inf)
            s2 = jnp.where(kr <= qc, s2, -jnp.inf)
        c1, c2 = carry
        return _softmax_step_t(s1, v, *c1), _softmax_step_t(s2, v, *c2)

    init = (jnp.full((1, tile), -jnp.inf, F32), jnp.zeros((1, tile), F32),
            jnp.zeros((ATTN_DV, tile), F32))
    carry = lax.fori_loop(0, qi, lambda ki, c: block(ki, c, False), (init, init))
    (_, l1, a1), (_, l2, a2) = block(qi, carry, True)
    lam = _lambda_value(lam_ref, lam_init)
    o = (a1 / l1 - lam * (a2 / l2)).T
    o = _rms(o, g_ref[...]) * (1.0 - lam_init)
    o_ref[...] = o.astype(BF16)


def attn_prompt(qb, kb, vb_t, lam_p, subln_g, *, batch, lam_init, tile=1024):
    m, hw = qb.shape
    seq = m // batch
    t = _row_tile(seq, tile)
    q3, k3 = (x.reshape(batch, seq, hw) for x in (qb, kb))
    q_spec = pl.BlockSpec((None, t, ATTN_HD), lambda b, h, qi: (b, qi, h))
    k_spec = pl.BlockSpec((None, seq, ATTN_HD), lambda b, h, qi: (b, 0, h))
    vt_spec = pl.BlockSpec((None, ATTN_DV, seq), lambda b, h, qi: (b, h, 0))
    const = lambda shape: pl.BlockSpec(shape, lambda b, h, qi: (0,) * len(shape))
    out = pl.pallas_call(
        functools.partial(_attn_prompt_kernel, tile=t, lam_init=lam_init),
        grid=(batch, ATTN_HEADS, seq // t),
        in_specs=[const((4, ATTN_DK)), const((1, ATTN_DV)), q_spec, k_spec, vt_spec],
        out_specs=q_spec,
        out_shape=jax.ShapeDtypeStruct((batch, seq, hw), BF16),
        compiler_params=_cparams("parallel", "parallel", "parallel"),
        name="attn_prompt",
    )(lam_p, subln_g.reshape(1, ATTN_DV), q3, k3, vb_t)
    return out.reshape(m, hw)


NEW_ROWS = 16


def _attn_decode_kernel(pt_ref, lam_ref, g_ref, q_ref, kn_ref, vn_ref, *refs, pages, n_q, lam_init):
    k_refs = refs[:pages]
    v_refs = refs[pages:2 * pages]
    o_ref, m_ref, l_ref, acc_ref = refs[2 * pages:]
    step = pl.program_id(1)
    rows = 2 * n_q
    page = k_refs[0].shape[0] // ATTN_HEADS
    head_rows = lambda ref, h: ref[pl.ds(h, page, stride=ATTN_HEADS), :].astype(BF16)

    @pl.when(step == 0)
    def _():
        m_ref[...] = jnp.full(m_ref.shape, -jnp.inf, F32)
        l_ref[...] = jnp.zeros(l_ref.shape, F32)
        acc_ref[...] = jnp.zeros(acc_ref.shape, F32)

    q_all = q_ref[...]
    q_heads = [q_all[h * rows:(h + 1) * rows].astype(BF16) for h in range(ATTN_HEADS)]

    def update(s, pv_fn):
        m_prev = m_ref[...]
        m_new = jnp.maximum(m_prev, jnp.max(s, axis=1, keepdims=True))
        alpha = jnp.exp(m_prev - m_new)
        p = jnp.exp(s - m_new)
        l_ref[...] = alpha * l_ref[...] + jnp.sum(p, axis=1, keepdims=True)
        acc_ref[...] = alpha * acc_ref[...] + pv_fn(p)
        m_ref[...] = m_new

    s = jnp.concatenate(
        [jnp.concatenate([_dot_nt(q_heads[h], head_rows(k_ref, h)) for h in range(ATTN_HEADS)], axis=0)
         for k_ref in k_refs], axis=1)

    def pv_pages(p):
        out = []
        for h in range(ATTN_HEADS):
            acc = None
            for i, v_ref in enumerate(v_refs):
                ph = p[h * rows:(h + 1) * rows, i * page:(i + 1) * page].astype(BF16)
                d = _dot(ph, head_rows(v_ref, h))
                acc = d if acc is None else acc + d
            out.append(acc)
        return jnp.concatenate(out, axis=0)

    update(s, pv_pages)

    @pl.when(step == pl.num_programs(1) - 1)
    def _():
        kn = kn_ref[...]
        vn = vn_ref[...]
        head_cols = lambda x, h: x[:, h * ATTN_HD:(h + 1) * ATTN_HD]
        s_new = jnp.concatenate([_dot_nt(q_heads[h], head_cols(kn, h)) for h in range(ATTN_HEADS)], axis=0)
        r = lax.broadcasted_iota(jnp.int32, s_new.shape, 0)
        c = lax.broadcasted_iota(jnp.int32, s_new.shape, 1)
        s_new = jnp.where(c <= r % n_q, s_new, -jnp.inf)

        def pv_new(p):
            return jnp.concatenate(
                [_dot(p[h * rows:(h + 1) * rows].astype(BF16), head_cols(vn, h)) for h in range(ATTN_HEADS)],
                axis=0)

        update(s_new, pv_new)
        lam = _lambda_value(lam_ref, lam_init)
        o_all = acc_ref[...] / l_ref[...]
        for h in range(ATTN_HEADS):
            r1 = h * rows
            o = o_all[r1:r1 + n_q] - lam * o_all[r1 + n_q:r1 + rows]
            o = _rms(o, g_ref[...]) * (1.0 - lam_init)
            o_ref[:, h * ATTN_DV:(h + 1) * ATTN_DV] = o.astype(BF16)


def attn_decode(qb, kb_new, vb_new, cache_k, cache_v, page_table, layer, lam_p, subln_g, *,
                lam_init, pages=8):
    b, n_q, hw = qb.shape
    n_layers, n_pool, page, n_heads, hd = cache_k.shape
    n_pages = page_table.shape[1]
    assert n_pages % pages == 0 and n_q <= NEW_ROWS and (n_heads, hd) == (ATTN_HEADS, ATTN_HD)
    n_rows = ATTN_HEADS * 2 * n_q
    ck = cache_k.reshape(n_layers * n_pool, page * n_heads, hd)
    cv = cache_v.reshape(n_layers * n_pool, page * n_heads, hd)
    pt = (page_table.astype(jnp.int32) + layer * n_pool).reshape(-1)
    half = (jnp.arange(ATTN_HD) // ATTN_DK)[None, :] == jnp.arange(2)[:, None]
    q4 = qb.reshape(b, n_q, ATTN_HEADS, ATTN_HD).astype(F32)
    qrows = jnp.where(half[None, None, :, None, :], jnp.transpose(q4, (0, 2, 1, 3))[:, :, None], 0.0)
    qrows = qrows.reshape(b, n_rows, ATTN_HD)
    pad = ((0, 0), (0, NEW_ROWS - n_q), (0, 0))
    kn = jnp.pad(kb_new, pad)
    vn = jnp.pad(vb_new, pad)

    def page_spec(i):
        return pl.BlockSpec((None, page * n_heads, hd),
                            lambda bi, s, pt_ref: (pt_ref[bi * n_pages + s * pages + i], 0, 0))

    per_b = lambda r, w: pl.BlockSpec((None, r, w), lambda bi, s, pt_ref: (bi, 0, 0))
    const = lambda shape: pl.BlockSpec(shape, lambda bi, s, pt_ref: (0,) * len(shape))
    grid_spec = pltpu.PrefetchScalarGridSpec(
        num_scalar_prefetch=1,
        grid=(b, n_pages // pages),
        in_specs=[const((4, ATTN_DK)), const((1, ATTN_DV)), per_b(n_rows, ATTN_HD), per_b(NEW_ROWS, hw),
                  per_b(NEW_ROWS, hw)] + [page_spec(i) for i in range(pages)] * 2,
        out_specs=per_b(n_q, hw),
        scratch_shapes=[pltpu.VMEM((n_rows, 1), F32), pltpu.VMEM((n_rows, 1), F32),
                        pltpu.VMEM((n_rows, ATTN_DV), F32)],
    )
    return pl.pallas_call(
        functools.partial(_attn_decode_kernel, pages=pages, n_q=n_q, lam_init=lam_init),
        grid_spec=grid_spec,
        out_shape=jax.ShapeDtypeStruct((b, n_q, hw), BF16),
        compiler_params=_cparams("parallel", "arbitrary"),
        name="attn_decode",
    )(pt, lam_p, subln_g.reshape(1, ATTN_DV), qrows, kn, vn, *([ck] * pages), *([cv] * pages))


def _ssm_in_kernel(h_ref, g_ref, wz_ref, wx_ref, wdt_ref, z_ref, xbc_ref, dt_ref):
    xn = _rms(h_ref[...], g_ref[...]).astype(BF16)
    z_ref[...] = _dot(xn, wz_ref[...])
    xbc_ref[...] = _dot(xn, wx_ref[...])
    dt_ref[...] = _dot(xn, wdt_ref[...])


def ssm_in_proj(h, g_mix, w_z, w_xbc, w_dt, *, tile=512):
    m, d = h.shape
    tm = _row_tile(m, tile)
    row = lambda w: pl.BlockSpec((tm, w), lambda i: (i, 0))
    return pl.pallas_call(
        _ssm_in_kernel,
        grid=(m // tm,),
        in_specs=[row(d), _const_spec((1, d)), _const_spec((d, SSM_INNER)),
                  _const_spec((d, SSM_CONV_DIM)), _const_spec((d, LANE))],
        out_specs=[row(SSM_INNER), row(SSM_CONV_DIM), row(LANE)],
        out_shape=[jax.ShapeDtypeStruct((m, SSM_INNER), F32),
                   jax.ShapeDtypeStruct((m, SSM_CONV_DIM), F32),
                   jax.ShapeDtypeStruct((m, LANE), F32)],
        compiler_params=_cparams("parallel"),
        name="ssm_in_proj",
    )(h, g_mix.reshape(1, d), w_z, w_xbc, w_dt)


def _split3(x):
    x1 = x.astype(BF16)
    r = x - x1.astype(F32)
    x2 = r.astype(BF16)
    x3 = (r - x2.astype(F32)).astype(BF16)
    return x1, x2, x3


def _spread_dot(x, sel):
    x1 = x.astype(BF16)
    x2 = (x - x1.astype(F32)).astype(BF16)
    return _dot(x1, sel) + _dot(x2, sel)


def _exact_dot_left(sel, x):
    x1, x2, x3 = _split3(x)
    return _dot(sel, x1) + _dot(sel, x2) + _dot(sel, x3)


def _ssd_kernel(z_ref, xbc_ref, dt_ref, cbuf_ref, h0_ref, cw_ref, cb_ref, dtb_ref, alog_ref,
                dskip_ref, ng_ref, spread_ref, y_ref, hf_ref, state_ref, tail_ref, *, valid):
    c = pl.program_id(1)
    q = xbc_ref.shape[0]

    @pl.when(c == 0)
    def _():
        state_ref[...] = h0_ref[...]
        tail_ref[...] = cbuf_ref[...]

    cur = xbc_ref[...]
    prev = tail_ref[...]
    sub = lax.broadcasted_iota(jnp.int32, prev.shape, 0)
    conv = cb_ref[...] + cur * cw_ref[SSM_CONV - 1:SSM_CONV, :]
    for tap in range(SSM_CONV - 1):
        shift = SSM_CONV - 1 - tap
        rolled = pltpu.roll(cur, shift, 0)
        head = jnp.where(sub < shift, pltpu.roll(prev, shift, 0), rolled[0:8])
        shifted = jnp.concatenate([head, rolled[8:]], axis=0)
        conv = conv + shifted * cw_ref[tap:tap + 1, :]
    tail_ref[...] = cur[q - 8:q]
    conv = conv * jax.nn.sigmoid(conv)
    xs = conv[:, :SSM_INNER]
    bm = conv[:, SSM_INNER:SSM_INNER + SSM_GN].astype(BF16)
    cm = conv[:, SSM_INNER + SSM_GN:].astype(BF16)

    row = lax.broadcasted_iota(jnp.int32, (q, q), 0)
    col = lax.broadcasted_iota(jnp.int32, (q, q), 1)
    tri = col <= row
    dt = jax.nn.softplus(dt_ref[...] + dtb_ref[...])
    if valid < q:
        dt = jnp.where(lax.broadcasted_iota(jnp.int32, dt.shape, 0) < valid, dt, 0.0)
    a = dt * (-jnp.exp(alog_ref[...]))
    acum = _exact_dot_left(tri.astype(BF16), a)
    acum_t = acum.T
    spread = spread_ref[...]
    dt_x = _spread_dot(dt, spread)
    grow_x = _spread_dot(jnp.exp(acum), spread)
    wst_x = _spread_dot(jnp.exp(acum[q - 1:q, :] - acum), spread)
    chunk_decay = jnp.exp(acum_t[:, q - 1:q])

    xd = xs * dt_x
    xd_b = xd.astype(BF16)
    xw_b = (xd * wst_x).astype(BF16)
    y_parts = []
    for g in range(SSM_GROUPS):
        ncols = slice(g * SSM_STATE, (g + 1) * SSM_STATE)
        cb = _dot_nt(cm[:, ncols], bm[:, ncols])
        heads = range(g * SSM_HPG, (g + 1) * SSM_HPG)
        s_g = jnp.concatenate([state_ref[h] for h in heads], axis=0)
        gcols = slice(g * SSM_HPG * SSM_HEAD_DIM, (g + 1) * SSM_HPG * SSM_HEAD_DIM)
        y_off = _dot_nt(cm[:, ncols], s_g.astype(BF16)) * grow_x[:, gcols]
        y_diag = []
        for h in heads:
            seg = acum[:, h:h + 1] - acum_t[h:h + 1, :]
            lmat = (cb * jnp.exp(jnp.where(tri, seg, -jnp.inf))).astype(BF16)
            y_diag.append(_dot(lmat, xd_b[:, h * SSM_HEAD_DIM:(h + 1) * SSM_HEAD_DIM]))
        y_parts.append(jnp.concatenate(y_diag, axis=1) + y_off)
        new_states = _dot_tn(xw_b[:, gcols], bm[:, ncols])
        for i, h in enumerate(heads):
            state_ref[h] = state_ref[h] * chunk_decay[h:h + 1, :] + new_states[i * SSM_HEAD_DIM:(i + 1) * SSM_HEAD_DIM, :]
    y = jnp.concatenate(y_parts, axis=1) + xs * dskip_ref[...]
    zf = z_ref[...]
    y = y * (zf * jax.nn.sigmoid(zf))
    gw = SSM_INNER // SSM_GROUPS
    out = []
    for g in range(SSM_GROUPS):
        yg = y[:, g * gw:(g + 1) * gw]
        out.append(_rms(yg, ng_ref[:, g * gw:(g + 1) * gw]))
    y_ref[...] = jnp.concatenate(out, axis=1).astype(BF16)

    @pl.when(c == pl.num_programs(1) - 1)
    def _():
        hf_ref[...] = state_ref[...]


def ssd_scan(z, xbc, dt_raw, conv_buf, h0, conv_w, conv_b, dt_bias, a_log, d_skip, norm_g, *,
             batch, chunk_len, valid):
    m = z.shape[0]
    seq = m // batch
    q = chunk_len
    assert seq % q == 0 and (valid == q or seq == q)
    nc = seq // q
    z3 = z.reshape(batch, seq, SSM_INNER)
    x3 = xbc.reshape(batch, seq, SSM_CONV_DIM)
    d3 = dt_raw.reshape(batch, seq, LANE)
    cbuf = jnp.pad(conv_buf, ((0, 0), (8 - (SSM_CONV - 1), 0), (0, 0)))
    spread = (jnp.arange(SSM_INNER)[None, :] // SSM_HEAD_DIM == jnp.arange(LANE)[:, None]).astype(BF16)
    lane_pad = lambda v: jnp.pad(v.reshape(1, -1), ((0, 0), (0, LANE - v.shape[-1])))
    chunk = lambda w: pl.BlockSpec((None, q, w), lambda b, c: (b, c, 0))
    state_spec = pl.BlockSpec((None, SSM_HEADS, SSM_HEAD_DIM, SSM_STATE), lambda b, c: (b, 0, 0, 0))
    const = lambda shape: pl.BlockSpec(shape, lambda b, c: (0,) * len(shape))
    y, hf = pl.pallas_call(
        functools.partial(_ssd_kernel, valid=valid),
        grid=(batch, nc),
        in_specs=[chunk(SSM_INNER), chunk(SSM_CONV_DIM), chunk(LANE),
                  pl.BlockSpec((None, 8, SSM_CONV_DIM), lambda b, c: (b, 0, 0)), state_spec,
                  const((SSM_CONV, SSM_CONV_DIM)), const((1, SSM_CONV_DIM)), const((1, LANE)),
                  const((1, LANE)), const((1, SSM_INNER)), const((1, SSM_INNER)),
                  const((LANE, SSM_INNER))],
        out_specs=[chunk(SSM_INNER), state_spec],
        out_shape=[jax.ShapeDtypeStruct((batch, seq, SSM_INNER), BF16),
                   jax.ShapeDtypeStruct(h0.shape, F32)],
        scratch_shapes=[pltpu.VMEM((SSM_HEADS, SSM_HEAD_DIM, SSM_STATE), F32),
                        pltpu.VMEM((8, SSM_CONV_DIM), F32)],
        compiler_params=_cparams("parallel", "arbitrary"),
        name="ssd_scan",
    )(z3, x3, d3, cbuf, h0, conv_w, conv_b.reshape(1, -1), lane_pad(dt_bias), lane_pad(a_log),
      jnp.repeat(d_skip, SSM_HEAD_DIM).reshape(1, -1), norm_g.reshape(1, -1), spread)
    return y.reshape(m, SSM_INNER), hf


def _ssm_group(h, batch, seq, conv_buf, h0, g_mix, w_z, w_xbc, w_dt, conv_w, conv_b, dt_bias, a_log,
               d_skip, norm_g):
    d = h.shape[1]
    if seq % SSM_CHUNK == 0:
        padded = seq
        chunk_len = valid = SSM_CHUNK
        hin = h
    else:
        assert seq <= SSM_SHORT_CHUNK
        padded = chunk_len = SSM_SHORT_CHUNK
        valid = seq
        hin = jnp.pad(h.reshape(batch, seq, d), ((0, 0), (0, padded - seq), (0, 0))).reshape(-1, d)
    z, xbc, dt = ssm_in_proj(hin, g_mix, w_z, w_xbc, w_dt)
    y, hf = ssd_scan(z, xbc, dt, conv_buf, h0, conv_w, conv_b, dt_bias, a_log, d_skip, norm_g,
                     batch=batch, chunk_len=chunk_len, valid=valid)
    xbc3 = xbc.reshape(batch, padded, SSM_CONV_DIM)[:, :seq]
    new_buf = jnp.concatenate([conv_buf, xbc3], axis=1)[:, seq:]
    y = y.reshape(batch, padded, SSM_INNER)[:, :seq].reshape(batch * seq, SSM_INNER)
    return y, new_buf, hf


def kernel(x_prompt, x_sample, cache_k, cache_v, state_ssm, state_conv, page_table, p_prompt, p_sample, norm_mix_g, norm_mlp_g, norm_ple_g, mlp_w_up, mlp_w_down, ple_w_gate, ple_w_proj, final_norm_g, sgu_w_in, sgu_ln_g, sgu_ln_b, sgu_w_s, sgu_b_s, sgu_w_out, attn_w_qkv, attn_lambda, attn_subln_g, attn_w_out, ssm_w_in, ssm_conv_w, ssm_conv_b, ssm_dt_bias, ssm_a_log, ssm_d, ssm_norm_g, ssm_w_out):
    b_p, l_p, d = x_prompt.shape
    b_s, l_s, _ = x_sample.shape
    depth = norm_mix_g.shape[0]
    past_len = page_table.shape[1] * cache_k.shape[2]
    m_s = b_s * l_s
    assert m_s % SGU_CHUNK == 0 and SGU_CHUNK % l_s == 0 and l_p % SGU_CHUNK == 0
    bf = lambda w: w.astype(BF16)

    hp = x_prompt.reshape(b_p * l_p, d)
    hs = x_sample.reshape(m_s, d)
    rope_p = _rope_tables(jnp.arange(l_p, dtype=jnp.int32))
    rope_s = _rope_tables(past_len + jnp.arange(l_s, dtype=jnp.int32))

    k_p, v_p, k_s, v_s = [], [], [], []
    ssm_p, conv_p, ssm_s, conv_s = [], [], [], []
    sgu_s = []
    for i in range(depth):
        kind, j = i % N_MIXERS, i // N_MIXERS
        g_mix = norm_mix_g[i]
        if kind == 0:
            w_in = bf(sgu_w_in[j])
            ws_p, bs_p = _sgu_spatial(sgu_w_s[j], sgu_b_s[j], l_p)
            ws_s, bs_s = _sgu_spatial(sgu_w_s[j], sgu_b_s[j], l_s)
            mix_p, _ = sgu_mix(hp, g_mix, w_in, sgu_ln_g[j], sgu_ln_b[j], ws_p, bs_p, want_v=False)
            mix_s, v_rows = sgu_mix(hs, g_mix, w_in, sgu_ln_g[j], sgu_ln_b[j], ws_s, bs_s,
                                    want_v=True, tile=SGU_CHUNK)
            sgu_s.append(v_rows.reshape(b_s, l_s, SGU_WIDTH))
            w_mix = bf(sgu_w_out[j])
        elif kind == 1:
            lam_init = 0.8 - 0.6 * math.exp(-0.3 * i)
            w_qkv = bf(attn_w_qkv[j])
            qb, k, v, kb, vb = qkv_rope(hp, g_mix, w_qkv, *rope_p, transposed_v=True)
            mix_p = attn_prompt(qb, kb, vb, attn_lambda[j], attn_subln_g[j], batch=b_p, lam_init=lam_init)
            k_p.append(k.reshape(b_p, l_p, ATTN_HEADS, ATTN_HD))
            v_p.append(v.reshape(b_p, l_p, ATTN_HEADS, ATTN_DV))
            qb, k, v, kb, vb = qkv_rope(hs, g_mix, w_qkv, *rope_s, tile=SGU_CHUNK)
            seq3 = lambda x: x.reshape(b_s, l_s, -1)
            mix_s = attn_decode(seq3(qb), seq3(kb), seq3(vb), cache_k, cache_v, page_table, j,
                                attn_lambda[j], attn_subln_g[j], lam_init=lam_init).reshape(m_s, -1)
            k_s.append(k.reshape(b_s, l_s, ATTN_HEADS, ATTN_HD))
            v_s.append(v.reshape(b_s, l_s, ATTN_HEADS, ATTN_DV))
            w_mix = bf(attn_w_out[j])
        else:
            w_in = ssm_w_in[j]
            w_z = bf(w_in[:, :SSM_INNER])
            w_xbc = bf(w_in[:, SSM_INNER:SSM_INNER + SSM_CONV_DIM])
            w_dt = bf(jnp.pad(w_in[:, SSM_INNER + SSM_CONV_DIM:], ((0, 0), (0, LANE - SSM_HEADS))))
            shared = (g_mix, w_z, w_xbc, w_dt, ssm_conv_w[j], ssm_conv_b[j], ssm_dt_bias[j],
                      ssm_a_log[j], ssm_d[j], ssm_norm_g[j])
            zero_buf = jnp.zeros((b_p, SSM_CONV - 1, SSM_CONV_DIM), F32)
            zero_h = jnp.zeros((b_p, SSM_HEADS, SSM_HEAD_DIM, SSM_STATE), F32)
            mix_p, cb_p, hf_p = _ssm_group(hp, b_p, l_p, zero_buf, zero_h, *shared)
            mix_s, cb_s, hf_s = _ssm_group(hs, b_s, l_s, state_conv[j], state_ssm[j], *shared)
            ssm_p.append(hf_p)
            conv_p.append(cb_p)
            ssm_s.append(hf_s)
            conv_s.append(cb_s)
            w_mix = bf(ssm_w_out[j])
        last = i == depth - 1
        weights = (w_mix, norm_mlp_g[i], bf(mlp_w_up[i]), bf(mlp_w_down[i]), norm_ple_g[i],
                   bf(ple_w_gate[i]), bf(ple_w_proj[i]), final_norm_g)
        hp = channel_update(hp, mix_p, p_prompt[i].reshape(b_p * l_p, D_PLE), *weights, final_norm=last)
        hs = channel_update(hs, mix_s, p_sample[i].reshape(m_s, D_PLE), *weights, final_norm=last,
                            tile=SGU_CHUNK)

    return (hp.reshape(b_p, l_p, d), hs.reshape(b_s, l_s, d),
            jnp.stack(k_p), jnp.stack(v_p), jnp.stack(k_s), jnp.stack(v_s),
            jnp.stack(ssm_p), jnp.stack(conv_p), jnp.stack(ssm_s), jnp.stack(conv_s),
            jnp.stack(sgu_s))
```

```python
import functools
import math

import jax
import jax.numpy as jnp
from jax import lax
from jax.experimental import pallas as pl
from jax.experimental.pallas import tpu as pltpu

F32 = jnp.float32
BF16 = jnp.bfloat16

EPS = 1e-6
D_MODEL = 1024
D_FF = 4 * D_MODEL
D_PLE = 256
N_MIXERS = 3

SGU_CHUNK = 128
SGU_WIDTH = 2 * D_MODEL
SGU_GROUPS = 8
SGU_GDIM = SGU_WIDTH // SGU_GROUPS

ATTN_HEADS = 8
ATTN_DK = 64
ATTN_DV = 128
ATTN_HD = 2 * ATTN_DK
ROPE_THETA = 10000.0

SSM_INNER = 2 * D_MODEL
SSM_HEAD_DIM = 64
SSM_HEADS = SSM_INNER // SSM_HEAD_DIM
SSM_GROUPS = 4
SSM_STATE = 128
SSM_CONV = 4
SSM_CHUNK = 128
SSM_SHORT_CHUNK = 16
SSM_GN = SSM_GROUPS * SSM_STATE
SSM_CONV_DIM = SSM_INNER + 2 * SSM_GN
SSM_HPG = SSM_HEADS // SSM_GROUPS

VMEM_LIMIT_BYTES = 56 * 1024 * 1024
LANE = 128


def _cparams(*sem):
    return pltpu.CompilerParams(dimension_semantics=sem, vmem_limit_bytes=VMEM_LIMIT_BYTES)


def _const_spec(shape):
    zeros = (0,) * len(shape)
    return pl.BlockSpec(shape, lambda *_: zeros, pipeline_mode=pl.Buffered(1))


def _layer_spec(stack, layer):
    tail = tuple(stack.shape[1:])
    zeros = (0,) * len(tail)
    return pl.BlockSpec((None,) + tail, lambda *_: (layer,) + zeros, pipeline_mode=pl.Buffered(1))


def _rms(x, g):
    ms = jnp.mean(x * x, axis=-1, keepdims=True)
    return (x * lax.rsqrt(ms + EPS)) * g


def _dot(a, b):
    return jnp.dot(a, b, preferred_element_type=F32)


def _dot_nt(a, b):
    return lax.dot_general(a, b, (((1,), (1,)), ((), ())), preferred_element_type=F32)


def _dot_tn(a, b):
    return lax.dot_general(a, b, (((0,), (0,)), ((), ())), preferred_element_type=F32)


def _row_tile(m, want):
    t = min(m, want)
    assert m % t == 0, (m, t)
    return t


def _channel_kernel(h_ref, mix_ref, p_ref, wmix_ref, gmlp_ref, wup_ref, wdown_ref, gple_ref,
                    wgate_ref, wproj_ref, gfin_ref, out_ref, *, final_norm):
    h = h_ref[...] + _dot(mix_ref[...], wmix_ref[...])
    xn = _rms(h, gmlp_ref[...]).astype(BF16)
    a = jnp.square(jnp.maximum(_dot(xn, wup_ref[...]), 0.0)).astype(BF16)
    h = h + _dot(a, wdown_ref[...])
    xn = _rms(h, gple_ref[...]).astype(BF16)
    gate = jax.nn.sigmoid(_dot(xn, wgate_ref[...]))
    h = h + gate * _dot(p_ref[...].astype(BF16), wproj_ref[...])
    if final_norm:
        h = _rms(h, gfin_ref[...])
    out_ref[...] = h


def channel_update(h, mix, p_all, layer, w_mix_all, mix_layer, g_mlp, w_up_all, w_down_all, g_ple,
                   w_gate_all, w_proj_all, g_fin, *, final_norm, tile=512):
    m, d = h.shape
    dm = mix.shape[1]
    tm = _row_tile(m, tile)
    row = lambda w: pl.BlockSpec((tm, w), lambda i: (i, 0))
    return pl.pallas_call(
        functools.partial(_channel_kernel, final_norm=final_norm),
        grid=(m // tm,),
        in_specs=[row(d), row(dm), pl.BlockSpec((None, tm, D_PLE), lambda i: (layer, i, 0)),
                  _layer_spec(w_mix_all, mix_layer), _const_spec((1, d)),
                  _layer_spec(w_up_all, layer), _layer_spec(w_down_all, layer), _const_spec((1, d)),
                  _layer_spec(w_gate_all, layer), _layer_spec(w_proj_all, layer), _const_spec((1, d))],
        out_specs=row(d),
        out_shape=jax.ShapeDtypeStruct((m, d), F32),
        compiler_params=_cparams("parallel"),
        name="channel_update",
    )(h, mix, p_all, w_mix_all, g_mlp.reshape(1, d), w_up_all, w_down_all, g_ple.reshape(1, d),
      w_gate_all, w_proj_all, g_fin.reshape(1, d))


def _sgu_kernel(h_ref, g_ref, win_ref, lng_ref, lnb_ref, ws_ref, bs_ref, mix_ref, *v_out,
                n_chunks):
    xn = _rms(h_ref[...], g_ref[...]).astype(BF16)
    uv = jax.nn.gelu(_dot(xn, win_ref[...]))
    u = uv[:, :SGU_WIDTH]
    v = uv[:, SGU_WIDTH:]
    mu = jnp.mean(v, axis=-1, keepdims=True)
    vc = v - mu
    v = (vc * lax.rsqrt(jnp.mean(vc * vc, axis=-1, keepdims=True) + EPS)) * lng_ref[...] + lnb_ref[...]
    if v_out:
        v_out[0][...] = v
    vb = v.astype(BF16)
    bs = bs_ref[...]
    for c in range(n_chunks):
        rows = slice(c * SGU_CHUNK, (c + 1) * SGU_CHUNK)
        for g in range(SGU_GROUPS):
            cols = slice(g * SGU_GDIM, (g + 1) * SGU_GDIM)
            s = _dot(ws_ref[g], vb[rows, cols]) + bs[:, g:g + 1]
            mix_ref[rows, cols] = (u[rows, cols] * s).astype(BF16)


def _sgu_spatial(w_s, b_s, seq):
    q = min(SGU_CHUNK, seq)
    tri = jnp.tril(jnp.ones((q, q), dtype=bool))
    ws = jnp.where(tri, w_s[:, :q, :q], 0)
    bs = b_s[:, :q]
    rep = SGU_CHUNK // q
    if rep > 1:
        eye = jnp.eye(rep, dtype=ws.dtype)
        ws = jnp.einsum("ab,gts->gatbs", eye, ws).reshape(SGU_GROUPS, SGU_CHUNK, SGU_CHUNK)
        bs = jnp.tile(bs, (1, rep))
    return ws.astype(BF16), bs.T


def sgu_mix(h, g_mix, w_in_all, layer, ln_g, ln_b, ws, bs, *, want_v, tile=256):
    m, d = h.shape
    tm = _row_tile(m, tile)
    row = lambda w: pl.BlockSpec((tm, w), lambda i: (i, 0))
    out_shape = [jax.ShapeDtypeStruct((m, SGU_WIDTH), BF16)]
    out_specs = [row(SGU_WIDTH)]
    if want_v:
        out_shape.append(jax.ShapeDtypeStruct((m, SGU_WIDTH), F32))
        out_specs.append(row(SGU_WIDTH))
    outs = pl.pallas_call(
        functools.partial(_sgu_kernel, n_chunks=tm // SGU_CHUNK),
        grid=(m // tm,),
        in_specs=[row(d), _const_spec((1, d)), _layer_spec(w_in_all, layer),
                  _const_spec((1, SGU_WIDTH)), _const_spec((1, SGU_WIDTH)),
                  _const_spec((SGU_GROUPS, SGU_CHUNK, SGU_CHUNK)),
                  _const_spec((SGU_CHUNK, SGU_GROUPS))],
        out_specs=out_specs,
        out_shape=out_shape,
        compiler_params=_cparams("parallel"),
        name="sgu_mix",
    )(h, g_mix.reshape(1, d), w_in_all, ln_g.reshape(1, -1), ln_b.reshape(1, -1), ws, bs)
    return outs if want_v else (outs[0], None)


def _rope_tables(pos):
    half = ATTN_DK // 2
    inv = ROPE_THETA ** (-jnp.arange(half, dtype=F32) / half)
    ang = pos.astype(F32)[:, None] * inv[None, :]
    cos, sin = jnp.cos(ang), jnp.sin(ang)
    cos_t = jnp.tile(cos, (1, 4))
    sin_t = jnp.tile(jnp.concatenate([-sin, sin], axis=1), (1, 2))
    return cos_t, sin_t


def _rope_head(x, cos, sin_signed, first_half):
    partner = jnp.where(first_half, pltpu.roll(x, LANE - ATTN_DK // 2, 1),
                        pltpu.roll(x, ATTN_DK // 2, 1))
    return x * cos + partner * sin_signed


def _qkv_kernel(h_ref, g_ref, w_ref, cos_ref, sin_ref, qb_ref, k_ref, v_ref, kb_ref, vb_ref, *,
                transposed_v):
    xn = _rms(h_ref[...], g_ref[...]).astype(BF16)
    qkv = _dot(xn, w_ref[...])
    cos = cos_ref[...]
    sin = sin_ref[...]
    lane = lax.broadcasted_iota(jnp.int32, cos.shape, 1)
    first_half = (lane % ATTN_DK) < (ATTN_DK // 2)
    qk_dim = ATTN_HEADS * ATTN_HD
    scale = ATTN_DK ** -0.5 * math.log2(math.e)
    rows = h_ref.shape[0]
    for hd in range(ATTN_HEADS):
        cols = slice(hd * ATTN_HD, (hd + 1) * ATTN_HD)
        q = _rope_head(qkv[:, cols], cos, sin, first_half)
        k = _rope_head(qkv[:, qk_dim + hd * ATTN_HD:qk_dim + (hd + 1) * ATTN_HD], cos, sin,
                       first_half)
        qb_ref[:, cols] = (q * scale).astype(BF16)
        k_ref[pl.ds(hd, rows, stride=ATTN_HEADS), :] = k
        kb_ref[:, cols] = k.astype(BF16)
        v_ref[pl.ds(hd, rows, stride=ATTN_HEADS), :] = qkv[:, 2 * qk_dim + hd * ATTN_DV:
                                                           2 * qk_dim + (hd + 1) * ATTN_DV]
    v = qkv[:, 2 * qk_dim:]
    vb_ref[...] = (v.T if transposed_v else v).astype(BF16)


def qkv_rope(h, g_mix, w_qkv_all, layer, cos_t, sin_t, *, transposed_v=False, tile=512):
    m, d = h.shape
    seq = cos_t.shape[0]
    tm = _row_tile(m, tile)
    row = lambda w: pl.BlockSpec((tm, w), lambda i: (i, 0))
    if seq % tm == 0:
        n_blk = seq // tm
        table = pl.BlockSpec((tm, LANE), lambda i: (i % n_blk, 0))
    else:
        assert tm % seq == 0, (tm, seq)
        cos_t = jnp.tile(cos_t, (tm // seq, 1))
        sin_t = jnp.tile(sin_t, (tm // seq, 1))
        table = _const_spec((tm, LANE))
    hw = ATTN_HEADS * ATTN_HD
    vb_spec, vb_shape = row(hw), jax.ShapeDtypeStruct((m, hw), BF16)
    head_rows = pl.BlockSpec((tm * ATTN_HEADS, ATTN_HD), lambda i: (i, 0))
    if transposed_v:
        assert seq % tm == 0
        n_blk = seq // tm
        vb_spec = pl.BlockSpec((None, hw, tm), lambda i: (i // n_blk, 0, i % n_blk))
        vb_shape = jax.ShapeDtypeStruct((m // seq, hw, seq), BF16)
    return pl.pallas_call(
        functools.partial(_qkv_kernel, transposed_v=transposed_v),
        grid=(m // tm,),
        in_specs=[row(d), _const_spec((1, d)), _layer_spec(w_qkv_all, layer), table, table],
        out_specs=[row(hw), head_rows, head_rows, row(hw), vb_spec],
        out_shape=[jax.ShapeDtypeStruct((m, hw), BF16),
                   jax.ShapeDtypeStruct((m * ATTN_HEADS, ATTN_HD), F32),
                   jax.ShapeDtypeStruct((m * ATTN_HEADS, ATTN_DV), F32),
                   jax.ShapeDtypeStruct((m, hw), BF16), vb_shape],
        compiler_params=_cparams("parallel"),
        name="qkv_rope",
    )(h, g_mix.reshape(1, d), w_qkv_all, cos_t, sin_t)


def _lambda_value(lam_ref, lam_init):
    lp = lam_ref[...]
    s1 = jnp.sum(lp[0:1] * lp[1:2], axis=-1, keepdims=True)
    s2 = jnp.sum(lp[2:3] * lp[3:4], axis=-1, keepdims=True)
    return jnp.exp(s1) - jnp.exp(s2) + lam_init


def _softmax_step_t(s, v, m_prev, l_prev, acc_prev):
    m_new = jnp.maximum(m_prev, jnp.max(s, axis=0, keepdims=True))
    alpha = jnp.exp2(m_prev - m_new)
    p = jnp.exp2(s - m_new)
    l_new = alpha * l_prev + jnp.sum(p, axis=0, keepdims=True)
    acc = alpha * acc_prev + _dot(v, p.astype(BF16))
    return m_new, l_new, acc


def _attn_prompt_kernel(lam_ref, g_ref, q_ref, k_ref, v_ref, o_ref, *, tile, lam_init):
    qi = pl.program_id(2)
    half = tile // 2
    q = q_ref[...]
    lane = lax.broadcasted_iota(jnp.int32, q.shape, 1)
    zero = jnp.zeros_like(q)
    q1 = jnp.where(lane < ATTN_DK, q, zero)
    q2 = jnp.where(lane < ATTN_DK, zero, q)

    def block(start, n_keys, q_lo, carry, masked):
        k = k_ref[pl.ds(start, n_keys), :]
        v = v_ref[:, pl.ds(start, n_keys)]
        s1 = _dot_nt(k, q1[q_lo:])
        s2 = _dot_nt(k, q2[q_lo:])
        if masked:
            kr = lax.broadcasted_iota(jnp.int32, s1.shape, 0)
            qc = lax.broadcasted_iota(jnp.int32, s1.shape, 1)
            s1 = jnp.where(kr <= qc, s1, -jnp.inf)
            s2 = jnp.where(kr <= qc, s2, -jnp.inf)
        c1, c2 = carry
        return _softmax_step_t(s1, v, *c1), _softmax_step_t(s2, v, *c2)

    init = (jnp.full((1, tile), -jnp.inf, F32), jnp.zeros((1, tile), F32),
            jnp.zeros((ATTN_DV, tile), F32))
    full = lambda ki, c: block(pl.multiple_of(ki * tile, tile), tile, 0, c, False)
    carry = lax.fori_loop(0, qi, full, (init, init))
    base = pl.multiple_of(qi * tile, tile)
    carry = block(base, half, 0, carry, True)
    tail = tuple(tuple(x[:, half:] for x in c) for c in carry)
    tail = block(base + half, half, half, tail, True)
    (_, l1, a1), (_, l2, a2) = tuple(
        tuple(jnp.concatenate([x[:, :half], y], axis=1) for x, y in zip(c, t)) for c, t in zip(carry, tail))
    lam = _lambda_value(lam_ref, lam_init)
    o = (a1 / l1 - lam * (a2 / l2)).T
    o = _rms(o, g_ref[...]) * (1.0 - lam_init)
    o_ref[...] = o.astype(BF16)


def attn_prompt(qb, kb, vb_t, lam_p, subln_g, *, batch, lam_init, tile=1024):
    m, hw = qb.shape
    seq = m // batch
    t = _row_tile(seq, tile)
    q3, k3 = (x.reshape(batch, seq, hw) for x in (qb, kb))
    q_spec = pl.BlockSpec((None, t, ATTN_HD), lambda b, h, qi: (b, qi, h))
    k_spec = pl.BlockSpec((None, seq, ATTN_HD), lambda b, h, qi: (b, 0, h))
    vt_spec = pl.BlockSpec((None, ATTN_DV, seq), lambda b, h, qi: (b, h, 0))
    const = lambda shape: pl.BlockSpec(shape, lambda b, h, qi: (0,) * len(shape))
    out = pl.pallas_call(
        functools.partial(_attn_prompt_kernel, tile=t, lam_init=lam_init),
        grid=(batch, ATTN_HEADS, seq // t),
        in_specs=[const((4, ATTN_DK)), const((1, ATTN_DV)), q_spec, k_spec, vt_spec],
        out_specs=q_spec,
        out_shape=jax.ShapeDtypeStruct((batch, seq, hw), BF16),
        compiler_params=_cparams("parallel", "parallel", "parallel"),
        name="attn_prompt",
    )(lam_p, subln_g.reshape(1, ATTN_DV), q3, k3, vb_t)
    return out.reshape(m, hw)


NEW_ROWS = 16


def _attn_decode_kernel(pt_ref, lam_ref, g_ref, q_ref, kn_ref, vn_ref, *refs, pages, n_q, lam_init):
    k_refs = refs[:pages]
    v_refs = refs[pages:2 * pages]
    o_ref, m_ref, l_ref, acc_ref = refs[2 * pages:]
    step = pl.program_id(1)
    rows = 2 * n_q

    @pl.when(step == 0)
    def _():
        m_ref[...] = jnp.full(m_ref.shape, -jnp.inf, F32)
        l_ref[...] = jnp.zeros(l_ref.shape, F32)
        acc_ref[...] = jnp.zeros(acc_ref.shape, F32)

    q_all = q_ref[...].astype(BF16)

    def update(s, pv_fn):
        m_prev = m_ref[...]
        m_new = jnp.maximum(m_prev, jnp.max(s, axis=1, keepdims=True))
        alpha = jnp.exp2(m_prev - m_new)
        p = jnp.exp2(s - m_new)
        l_ref[...] = alpha * l_ref[...] + jnp.sum(p, axis=1, keepdims=True)
        acc_ref[...] = alpha * acc_ref[...] + pv_fn(p)
        m_ref[...] = m_new

    n_cols = k_refs[0].shape[0]
    r = lax.broadcasted_iota(jnp.int32, (q_all.shape[0], n_cols), 0)
    c = lax.broadcasted_iota(jnp.int32, (q_all.shape[0], n_cols), 1)
    own_head = (c % ATTN_HEADS) == (r // rows)
    s = jnp.concatenate(
        [jnp.where(own_head, _dot_nt(q_all, k_ref[...].astype(BF16)), -jnp.inf) for k_ref in k_refs],
        axis=1)

    def pv_pages(p):
        acc = None
        for i, v_ref in enumerate(v_refs):
            d = _dot(p[:, i * n_cols:(i + 1) * n_cols].astype(BF16), v_ref[...].astype(BF16))
            acc = d if acc is None else acc + d
        return acc

    update(s, pv_pages)

    @pl.when(step == pl.num_programs(1) - 1)
    def _():
        kn = kn_ref[...]
        vn = vn_ref[...]
        head_cols = lambda x, h: x[:, h * ATTN_HD:(h + 1) * ATTN_HD]
        q_heads = [q_all[h * rows:(h + 1) * rows] for h in range(ATTN_HEADS)]
        s_new = jnp.concatenate([_dot_nt(q_heads[h], head_cols(kn, h)) for h in range(ATTN_HEADS)], axis=0)
        rn = lax.broadcasted_iota(jnp.int32, s_new.shape, 0)
        cn = lax.broadcasted_iota(jnp.int32, s_new.shape, 1)
        s_new = jnp.where(cn <= rn % n_q, s_new, -jnp.inf)

        def pv_new(p):
            return jnp.concatenate(
                [_dot(p[h * rows:(h + 1) * rows].astype(BF16), head_cols(vn, h)) for h in range(ATTN_HEADS)],
                axis=0)

        update(s_new, pv_new)
        lam = _lambda_value(lam_ref, lam_init)
        o_all = acc_ref[...] / l_ref[...]
        for h in range(ATTN_HEADS):
            r1 = h * rows
            o = o_all[r1:r1 + n_q] - lam * o_all[r1 + n_q:r1 + rows]
            o = _rms(o, g_ref[...]) * (1.0 - lam_init)
            o_ref[:, h * ATTN_DV:(h + 1) * ATTN_DV] = o.astype(BF16)


def attn_decode(qb, kb_new, vb_new, cache_k, cache_v, page_table, layer, lam_p, subln_g, *,
                lam_init, pages=8):
    b, n_q, hw = qb.shape
    n_layers, n_pool, page, n_heads, hd = cache_k.shape
    n_pages = page_table.shape[1]
    assert n_pages % pages == 0 and n_q <= NEW_ROWS and (n_heads, hd) == (ATTN_HEADS, ATTN_HD)
    n_rows = ATTN_HEADS * 2 * n_q
    ck = cache_k.reshape(n_layers * n_pool, page * n_heads, hd)
    cv = cache_v.reshape(n_layers * n_pool, page * n_heads, hd)
    pt = (page_table.astype(jnp.int32) + layer * n_pool).reshape(-1)
    half = (jnp.arange(ATTN_HD) // ATTN_DK)[None, :] == jnp.arange(2)[:, None]
    q4 = qb.reshape(b, n_q, ATTN_HEADS, ATTN_HD).astype(F32)
    qrows = jnp.where(half[None, None, :, None, :], jnp.transpose(q4, (0, 2, 1, 3))[:, :, None], 0.0)
    qrows = qrows.reshape(b, n_rows, ATTN_HD)
    pad = ((0, 0), (0, NEW_ROWS - n_q), (0, 0))
    kn = jnp.pad(kb_new, pad)
    vn = jnp.pad(vb_new, pad)

    def page_spec(i):
        return pl.BlockSpec((None, page * n_heads, hd),
                            lambda bi, s, pt_ref: (pt_ref[bi * n_pages + s * pages + i], 0, 0))

    per_b = lambda r, w: pl.BlockSpec((None, r, w), lambda bi, s, pt_ref: (bi, 0, 0))
    const = lambda shape: pl.BlockSpec(shape, lambda bi, s, pt_ref: (0,) * len(shape))
    grid_spec = pltpu.PrefetchScalarGridSpec(
        num_scalar_prefetch=1,
        grid=(b, n_pages // pages),
        in_specs=[const((4, ATTN_DK)), const((1, ATTN_DV)), per_b(n_rows, ATTN_HD), per_b(NEW_ROWS, hw),
                  per_b(NEW_ROWS, hw)] + [page_spec(i) for i in range(pages)] * 2,
        out_specs=per_b(n_q, hw),
        scratch_shapes=[pltpu.VMEM((n_rows, 1), F32), pltpu.VMEM((n_rows, 1), F32),
                        pltpu.VMEM((n_rows, ATTN_DV), F32)],
    )
    return pl.pallas_call(
        functools.partial(_attn_decode_kernel, pages=pages, n_q=n_q, lam_init=lam_init),
        grid_spec=grid_spec,
        out_shape=jax.ShapeDtypeStruct((b, n_q, hw), BF16),
        compiler_params=_cparams("parallel", "arbitrary"),
        name="attn_decode",
    )(pt, lam_p, subln_g.reshape(1, ATTN_DV), qrows, kn, vn, *([ck] * pages), *([cv] * pages))


def _ssm_in_kernel(h_ref, g_ref, w_ref, wdt_ref, z_ref, xbc_ref, dt_ref):
    xn = _rms(h_ref[...], g_ref[...]).astype(BF16)
    z_ref[...] = _dot(xn, w_ref[:, :SSM_INNER])
    xbc_ref[...] = _dot(xn, w_ref[:, SSM_INNER:SSM_INNER + SSM_CONV_DIM])
    dt_ref[...] = _dot(xn, wdt_ref[...])


def ssm_in_proj(h, g_mix, w_in_all, layer, w_dt, *, tile=512):
    m, d = h.shape
    tm = _row_tile(m, tile)
    row = lambda w: pl.BlockSpec((tm, w), lambda i: (i, 0))
    return pl.pallas_call(
        _ssm_in_kernel,
        grid=(m // tm,),
        in_specs=[row(d), _const_spec((1, d)), _layer_spec(w_in_all, layer), _const_spec((d, LANE))],
        out_specs=[row(SSM_INNER), row(SSM_CONV_DIM), row(LANE)],
        out_shape=[jax.ShapeDtypeStruct((m, SSM_INNER), F32),
                   jax.ShapeDtypeStruct((m, SSM_CONV_DIM), F32),
                   jax.ShapeDtypeStruct((m, LANE), F32)],
        compiler_params=_cparams("parallel"),
        name="ssm_in_proj",
    )(h, g_mix.reshape(1, d), w_in_all, w_dt)


def _split3(x):
    x1 = x.astype(BF16)
    r = x - x1.astype(F32)
    x2 = r.astype(BF16)
    x3 = (r - x2.astype(F32)).astype(BF16)
    return x1, x2, x3


def _spread_dot(x, sel):
    x1 = x.astype(BF16)
    x2 = (x - x1.astype(F32)).astype(BF16)
    return _dot(x1, sel) + _dot(x2, sel)


def _exact_dot_left(sel, x):
    x1, x2, x3 = _split3(x)
    return _dot(sel, x1) + _dot(sel, x2) + _dot(sel, x3)


def _ssd_kernel(z_ref, xbc_ref, dt_ref, cbuf_ref, h0_ref, cw_ref, cb_ref, dtb_ref, alog_ref,
                dskip_ref, ng_ref, spread_ref, y_ref, hf_ref, state_ref, tail_ref, *, valid):
    c = pl.program_id(1)
    q = xbc_ref.shape[0]

    @pl.when(c == 0)
    def _():
        state_ref[...] = h0_ref[...]
        tail_ref[...] = cbuf_ref[...]

    cur = xbc_ref[...]
    prev = tail_ref[...]
    sub = lax.broadcasted_iota(jnp.int32, prev.shape, 0)
    conv = cb_ref[...] + cur * cw_ref[SSM_CONV - 1:SSM_CONV, :]
    for tap in range(SSM_CONV - 1):
        shift = SSM_CONV - 1 - tap
        rolled = pltpu.roll(cur, shift, 0)
        head = jnp.where(sub < shift, pltpu.roll(prev, shift, 0), rolled[0:8])
        shifted = jnp.concatenate([head, rolled[8:]], axis=0)
        conv = conv + shifted * cw_ref[tap:tap + 1, :]
    tail_ref[...] = cur[q - 8:q]
    conv = conv * jax.nn.sigmoid(conv)
    xs = conv[:, :SSM_INNER]
    bm = conv[:, SSM_INNER:SSM_INNER + SSM_GN].astype(BF16)
    cm = conv[:, SSM_INNER + SSM_GN:].astype(BF16)

    row = lax.broadcasted_iota(jnp.int32, (q, q), 0)
    col = lax.broadcasted_iota(jnp.int32, (q, q), 1)
    tri = col <= row
    dt = jax.nn.softplus(dt_ref[...] + dtb_ref[...])
    if valid < q:
        dt = jnp.where(lax.broadcasted_iota(jnp.int32, dt.shape, 0) < valid, dt, 0.0)
    a = dt * (-jnp.exp(alog_ref[...]))
    acum = _exact_dot_left(tri.astype(BF16), a)
    acum_t = acum.T
    spread = spread_ref[...]
    dt_x = _spread_dot(dt, spread)
    grow_x = _spread_dot(jnp.exp(acum), spread)
    wst_x = _spread_dot(jnp.exp(acum[q - 1:q, :] - acum), spread)
    chunk_decay = jnp.exp(acum_t[:, q - 1:q])

    xd = xs * dt_x
    xd_b = xd.astype(BF16)
    xw_b = (xd * wst_x).astype(BF16)
    y_parts = []
    for g in range(SSM_GROUPS):
        ncols = slice(g * SSM_STATE, (g + 1) * SSM_STATE)
        cb = _dot_nt(cm[:, ncols], bm[:, ncols])
        heads = range(g * SSM_HPG, (g + 1) * SSM_HPG)
        s_g = jnp.concatenate([state_ref[h] for h in heads], axis=0)
        gcols = slice(g * SSM_HPG * SSM_HEAD_DIM, (g + 1) * SSM_HPG * SSM_HEAD_DIM)
        y_off = _dot_nt(cm[:, ncols], s_g.astype(BF16)) * grow_x[:, gcols]
        y_diag = []
        for h in heads:
            seg = acum[:, h:h + 1] - acum_t[h:h + 1, :]
            lmat = (cb * jnp.exp(jnp.where(tri, seg, -jnp.inf))).astype(BF16)
            y_diag.append(_dot(lmat, xd_b[:, h * SSM_HEAD_DIM:(h + 1) * SSM_HEAD_DIM]))
        y_parts.append(jnp.concatenate(y_diag, axis=1) + y_off)
        new_states = _dot_tn(xw_b[:, gcols], bm[:, ncols])
        for i, h in enumerate(heads):
            state_ref[h] = state_ref[h] * chunk_decay[h:h + 1, :] + new_states[i * SSM_HEAD_DIM:(i + 1) * SSM_HEAD_DIM, :]
    y = jnp.concatenate(y_parts, axis=1) + xs * dskip_ref[...]
    zf = z_ref[...]
    y = y * (zf * jax.nn.sigmoid(zf))
    gw = SSM_INNER // SSM_GROUPS
    out = []
    for g in range(SSM_GROUPS):
        yg = y[:, g * gw:(g + 1) * gw]
        out.append(_rms(yg, ng_ref[:, g * gw:(g + 1) * gw]))
    y_ref[...] = jnp.concatenate(out, axis=1).astype(BF16)

    @pl.when(c == pl.num_programs(1) - 1)
    def _():
        hf_ref[...] = state_ref[...]


def ssd_scan(z, xbc, dt_raw, conv_buf, h0, conv_w, conv_b, dt_bias, a_log, d_skip, norm_g, *,
             batch, chunk_len, valid):
    m = z.shape[0]
    seq = m // batch
    q = chunk_len
    assert seq % q == 0 and (valid == q or seq == q)
    nc = seq // q
    z3 = z.reshape(batch, seq, SSM_INNER)
    x3 = xbc.reshape(batch, seq, SSM_CONV_DIM)
    d3 = dt_raw.reshape(batch, seq, LANE)
    cbuf = jnp.pad(conv_buf, ((0, 0), (8 - (SSM_CONV - 1), 0), (0, 0)))
    spread = (jnp.arange(SSM_INNER)[None, :] // SSM_HEAD_DIM == jnp.arange(LANE)[:, None]).astype(BF16)
    lane_pad = lambda v: jnp.pad(v.reshape(1, -1), ((0, 0), (0, LANE - v.shape[-1])))
    chunk = lambda w: pl.BlockSpec((None, q, w), lambda b, c: (b, c, 0))
    state_spec = pl.BlockSpec((None, SSM_HEADS, SSM_HEAD_DIM, SSM_STATE), lambda b, c: (b, 0, 0, 0))
    const = lambda shape: pl.BlockSpec(shape, lambda b, c: (0,) * len(shape))
    y, hf = pl.pallas_call(
        functools.partial(_ssd_kernel, valid=valid),
        grid=(batch, nc),
        in_specs=[chunk(SSM_INNER), chunk(SSM_CONV_DIM), chunk(LANE),
                  pl.BlockSpec((None, 8, SSM_CONV_DIM), lambda b, c: (b, 0, 0)), state_spec,
                  const((SSM_CONV, SSM_CONV_DIM)), const((1, SSM_CONV_DIM)), const((1, LANE)),
                  const((1, LANE)), const((1, SSM_INNER)), const((1, SSM_INNER)),
                  const((LANE, SSM_INNER))],
        out_specs=[chunk(SSM_INNER), state_spec],
        out_shape=[jax.ShapeDtypeStruct((batch, seq, SSM_INNER), BF16),
                   jax.ShapeDtypeStruct(h0.shape, F32)],
        scratch_shapes=[pltpu.VMEM((SSM_HEADS, SSM_HEAD_DIM, SSM_STATE), F32),
                        pltpu.VMEM((8, SSM_CONV_DIM), F32)],
        compiler_params=_cparams("parallel", "arbitrary"),
        name="ssd_scan",
    )(z3, x3, d3, cbuf, h0, conv_w, conv_b.reshape(1, -1), lane_pad(dt_bias), lane_pad(a_log),
      jnp.repeat(d_skip, SSM_HEAD_DIM).reshape(1, -1), norm_g.reshape(1, -1), spread)
    return y.reshape(m, SSM_INNER), hf


def _ssm_group(h, batch, seq, conv_buf, h0, g_mix, w_in_all, layer, w_dt, conv_w, conv_b, dt_bias, a_log,
               d_skip, norm_g):
    d = h.shape[1]
    if seq % SSM_CHUNK == 0:
        padded = seq
        chunk_len = valid = SSM_CHUNK
        hin = h
    else:
        assert seq <= SSM_SHORT_CHUNK
        padded = chunk_len = SSM_SHORT_CHUNK
        valid = seq
        hin = jnp.pad(h.reshape(batch, seq, d), ((0, 0), (0, padded - seq), (0, 0))).reshape(-1, d)
    z, xbc, dt = ssm_in_proj(hin, g_mix, w_in_all, layer, w_dt)
    y, hf = ssd_scan(z, xbc, dt, conv_buf, h0, conv_w, conv_b, dt_bias, a_log, d_skip, norm_g,
                     batch=batch, chunk_len=chunk_len, valid=valid)
    xbc3 = xbc.reshape(batch, padded, SSM_CONV_DIM)[:, :seq]
    new_buf = jnp.concatenate([conv_buf, xbc3], axis=1)[:, seq:]
    y = y.reshape(batch, padded, SSM_INNER)[:, :seq].reshape(batch * seq, SSM_INNER)
    return y, new_buf, hf


def kernel(x_prompt, x_sample, cache_k, cache_v, state_ssm, state_conv, page_table, p_prompt, p_sample, norm_mix_g, norm_mlp_g, norm_ple_g, mlp_w_up, mlp_w_down, ple_w_gate, ple_w_proj, final_norm_g, sgu_w_in, sgu_ln_g, sgu_ln_b, sgu_w_s, sgu_b_s, sgu_w_out, attn_w_qkv, attn_lambda, attn_subln_g, attn_w_out, ssm_w_in, ssm_conv_w, ssm_conv_b, ssm_dt_bias, ssm_a_log, ssm_d, ssm_norm_g, ssm_w_out):
    b_p, l_p, d = x_prompt.shape
    b_s, l_s, _ = x_sample.shape
    depth = norm_mix_g.shape[0]
    past_len = page_table.shape[1] * cache_k.shape[2]
    m_s = b_s * l_s
    assert m_s % SGU_CHUNK == 0 and SGU_CHUNK % l_s == 0 and l_p % SGU_CHUNK == 0
    bf = lambda w: w.astype(BF16)

    hp = x_prompt.reshape(b_p * l_p, d)
    hs = x_sample.reshape(m_s, d)
    rope_p = _rope_tables(jnp.arange(l_p, dtype=jnp.int32))
    rope_s = _rope_tables(past_len + jnp.arange(l_s, dtype=jnp.int32))

    w_up, w_down, w_gate, w_proj = bf(mlp_w_up), bf(mlp_w_down), bf(ple_w_gate), bf(ple_w_proj)
    w_sgu_in, w_sgu_out = bf(sgu_w_in), bf(sgu_w_out)
    w_qkv, w_attn_out = bf(attn_w_qkv), bf(attn_w_out)
    w_ssm_in, w_ssm_out = bf(ssm_w_in), bf(ssm_w_out)
    pp = p_prompt.reshape(depth, b_p * l_p, D_PLE)
    ps = p_sample.reshape(depth, m_s, D_PLE)

    k_p, v_p, k_s, v_s = [], [], [], []
    ssm_p, conv_p, ssm_s, conv_s = [], [], [], []
    sgu_s = []
    for i in range(depth):
        kind, j = i % N_MIXERS, i // N_MIXERS
        g_mix = norm_mix_g[i]
        if kind == 0:
            ws_p, bs_p = _sgu_spatial(sgu_w_s[j], sgu_b_s[j], l_p)
            ws_s, bs_s = _sgu_spatial(sgu_w_s[j], sgu_b_s[j], l_s)
            mix_p, _ = sgu_mix(hp, g_mix, w_sgu_in, j, sgu_ln_g[j], sgu_ln_b[j], ws_p, bs_p, want_v=False)
            mix_s, v_rows = sgu_mix(hs, g_mix, w_sgu_in, j, sgu_ln_g[j], sgu_ln_b[j], ws_s, bs_s,
                                    want_v=True, tile=SGU_CHUNK)
            sgu_s.append(v_rows.reshape(b_s, l_s, SGU_WIDTH))
            w_mix = w_sgu_out
        elif kind == 1:
            lam_init = 0.8 - 0.6 * math.exp(-0.3 * i)
            qb, k, v, kb, vb = qkv_rope(hp, g_mix, w_qkv, j, *rope_p, transposed_v=True)
            mix_p = attn_prompt(qb, kb, vb, attn_lambda[j], attn_subln_g[j], batch=b_p, lam_init=lam_init)
            k_p.append(k.reshape(b_p, l_p, ATTN_HEADS, ATTN_HD))
            v_p.append(v.reshape(b_p, l_p, ATTN_HEADS, ATTN_DV))
            qb, k, v, kb, vb = qkv_rope(hs, g_mix, w_qkv, j, *rope_s, tile=SGU_CHUNK)
            seq3 = lambda x: x.reshape(b_s, l_s, -1)
            mix_s = attn_decode(seq3(qb), seq3(kb), seq3(vb), cache_k, cache_v, page_table, j,
                                attn_lambda[j], attn_subln_g[j], lam_init=lam_init).reshape(m_s, -1)
            k_s.append(k.reshape(b_s, l_s, ATTN_HEADS, ATTN_HD))
            v_s.append(v.reshape(b_s, l_s, ATTN_HEADS, ATTN_DV))
            w_mix = w_attn_out
        else:
            w_dt = jnp.pad(w_ssm_in[j][:, SSM_INNER + SSM_CONV_DIM:], ((0, 0), (0, LANE - SSM_HEADS)))
            shared = (g_mix, w_ssm_in, j, w_dt, ssm_conv_w[j], ssm_conv_b[j], ssm_dt_bias[j],
                      ssm_a_log[j], ssm_d[j], ssm_norm_g[j])
            zero_buf = jnp.zeros((b_p, SSM_CONV - 1, SSM_CONV_DIM), F32)
            zero_h = jnp.zeros((b_p, SSM_HEADS, SSM_HEAD_DIM, SSM_STATE), F32)
            mix_p, cb_p, hf_p = _ssm_group(hp, b_p, l_p, zero_buf, zero_h, *shared)
            mix_s, cb_s, hf_s = _ssm_group(hs, b_s, l_s, state_conv[j], state_ssm[j], *shared)
            ssm_p.append(hf_p)
            conv_p.append(cb_p)
            ssm_s.append(hf_s)
            conv_s.append(cb_s)
            w_mix = w_ssm_out
        last = i == depth - 1
        weights = (w_mix, j, norm_mlp_g[i], w_up, w_down, norm_ple_g[i], w_gate, w_proj, final_norm_g)
        hp = channel_update(hp, mix_p, pp, i, *weights, final_norm=last)
        hs = channel_update(hs, mix_s, ps, i, *weights, final_norm=last, tile=SGU_CHUNK)

    return (hp.reshape(b_p, l_p, d), hs.reshape(b_s, l_s, d),
            jnp.stack(k_p), jnp.stack(v_p), jnp.stack(k_s), jnp.stack(v_s),
            jnp.stack(ssm_p), jnp.stack(conv_p), jnp.stack(ssm_s), jnp.stack(conv_s),
            jnp.stack(sgu_s))
```

```python
import functools
import math

import jax
import jax.numpy as jnp
from jax import lax
from jax.experimental import pallas as pl
from jax.experimental.pallas import tpu as pltpu

F32 = jnp.float32
BF16 = jnp.bfloat16

EPS = 1e-6
D_MODEL = 1024
D_FF = 4 * D_MODEL
D_PLE = 256
N_MIXERS = 3

SGU_CHUNK = 128
SGU_WIDTH = 2 * D_MODEL
SGU_GROUPS = 8
SGU_GDIM = SGU_WIDTH // SGU_GROUPS

ATTN_HEADS = 8
ATTN_DK = 64
ATTN_DV = 128
ATTN_HD = 2 * ATTN_DK
ROPE_THETA = 10000.0

SSM_INNER = 2 * D_MODEL
SSM_HEAD_DIM = 64
SSM_HEADS = SSM_INNER // SSM_HEAD_DIM
SSM_GROUPS = 4
SSM_STATE = 128
SSM_CONV = 4
SSM_CHUNK = 128
SSM_SHORT_CHUNK = 16
SSM_GN = SSM_GROUPS * SSM_STATE
SSM_CONV_DIM = SSM_INNER + 2 * SSM_GN
SSM_HPG = SSM_HEADS // SSM_GROUPS

VMEM_LIMIT_BYTES = 56 * 1024 * 1024
LANE = 128


def _cparams(*sem):
    return pltpu.CompilerParams(dimension_semantics=sem, vmem_limit_bytes=VMEM_LIMIT_BYTES)


def _const_spec(shape):
    zeros = (0,) * len(shape)
    return pl.BlockSpec(shape, lambda *_: zeros, pipeline_mode=pl.Buffered(1))


def _layer_spec(stack, layer):
    tail = tuple(stack.shape[1:])
    zeros = (0,) * len(tail)
    return pl.BlockSpec((None,) + tail, lambda *_: (layer,) + zeros, pipeline_mode=pl.Buffered(1))


def _rms(x, g):
    ms = jnp.mean(x * x, axis=-1, keepdims=True)
    return (x * lax.rsqrt(ms + EPS)) * g


def _dot(a, b):
    return jnp.dot(a, b, preferred_element_type=F32)


def _dot_nt(a, b):
    return lax.dot_general(a, b, (((1,), (1,)), ((), ())), preferred_element_type=F32)


def _dot_tn(a, b):
    return lax.dot_general(a, b, (((0,), (0,)), ((), ())), preferred_element_type=F32)


def _row_tile(m, want):
    t = min(m, want)
    assert m % t == 0, (m, t)
    return t


def _channel_kernel(h_ref, mix_ref, p_ref, wmix_ref, gmlp_ref, wup_ref, wdown_ref, gple_ref,
                    wgate_ref, wproj_ref, gfin_ref, out_ref, *, final_norm):
    h = h_ref[...] + _dot(mix_ref[...], wmix_ref[...])
    xn = _rms(h, gmlp_ref[...]).astype(BF16)
    a = jnp.square(jnp.maximum(_dot(xn, wup_ref[...]), 0.0)).astype(BF16)
    h = h + _dot(a, wdown_ref[...])
    xn = _rms(h, gple_ref[...]).astype(BF16)
    gate = jax.nn.sigmoid(_dot(xn, wgate_ref[...]))
    h = h + gate * _dot(p_ref[...].astype(BF16), wproj_ref[...])
    if final_norm:
        h = _rms(h, gfin_ref[...])
    out_ref[...] = h


def channel_update(h, mix, p_all, layer, w_mix_all, mix_layer, g_mlp, w_up_all, w_down_all, g_ple,
                   w_gate_all, w_proj_all, g_fin, *, final_norm, tile=512):
    m, d = h.shape
    dm = mix.shape[1]
    tm = _row_tile(m, tile)
    row = lambda w: pl.BlockSpec((tm, w), lambda i: (i, 0))
    return pl.pallas_call(
        functools.partial(_channel_kernel, final_norm=final_norm),
        grid=(m // tm,),
        in_specs=[row(d), row(dm), pl.BlockSpec((None, tm, D_PLE), lambda i: (layer, i, 0)),
                  _layer_spec(w_mix_all, mix_layer), _const_spec((1, d)),
                  _layer_spec(w_up_all, layer), _layer_spec(w_down_all, layer), _const_spec((1, d)),
                  _layer_spec(w_gate_all, layer), _layer_spec(w_proj_all, layer), _const_spec((1, d))],
        out_specs=row(d),
        out_shape=jax.ShapeDtypeStruct((m, d), F32),
        compiler_params=_cparams("parallel"),
        name="channel_update",
    )(h, mix, p_all, w_mix_all, g_mlp.reshape(1, d), w_up_all, w_down_all, g_ple.reshape(1, d),
      w_gate_all, w_proj_all, g_fin.reshape(1, d))


def _sgu_kernel(h_ref, g_ref, win_ref, lng_ref, lnb_ref, ws_ref, bs_ref, mix_ref, *v_out,
                n_chunks):
    xn = _rms(h_ref[...], g_ref[...]).astype(BF16)
    v = jax.nn.gelu(_dot(xn, win_ref[:, SGU_WIDTH:]))
    u = jax.nn.gelu(_dot(xn, win_ref[:, :SGU_WIDTH]))
    mu = jnp.mean(v, axis=-1, keepdims=True)
    vc = v - mu
    v = (vc * lax.rsqrt(jnp.mean(vc * vc, axis=-1, keepdims=True) + EPS)) * lng_ref[...] + lnb_ref[...]
    if v_out:
        v_out[0][...] = v
    vb = v.astype(BF16)
    bs = bs_ref[...]
    for c in range(n_chunks):
        rows = slice(c * SGU_CHUNK, (c + 1) * SGU_CHUNK)
        for g in range(SGU_GROUPS):
            cols = slice(g * SGU_GDIM, (g + 1) * SGU_GDIM)
            s = _dot(ws_ref[g], vb[rows, cols]) + bs[:, g:g + 1]
            mix_ref[rows, cols] = (u[rows, cols] * s).astype(BF16)


def _sgu_spatial(w_s, b_s, seq):
    q = min(SGU_CHUNK, seq)
    tri = jnp.tril(jnp.ones((q, q), dtype=bool))
    ws = jnp.where(tri, w_s[:, :q, :q], 0)
    bs = b_s[:, :q]
    rep = SGU_CHUNK // q
    if rep > 1:
        eye = jnp.eye(rep, dtype=ws.dtype)
        ws = jnp.einsum("ab,gts->gatbs", eye, ws).reshape(SGU_GROUPS, SGU_CHUNK, SGU_CHUNK)
        bs = jnp.tile(bs, (1, rep))
    return ws.astype(BF16), bs.T


def sgu_mix(h, g_mix, w_in_all, layer, ln_g, ln_b, ws, bs, *, want_v, tile=256):
    m, d = h.shape
    tm = _row_tile(m, tile)
    row = lambda w: pl.BlockSpec((tm, w), lambda i: (i, 0))
    out_shape = [jax.ShapeDtypeStruct((m, SGU_WIDTH), BF16)]
    out_specs = [row(SGU_WIDTH)]
    if want_v:
        out_shape.append(jax.ShapeDtypeStruct((m, SGU_WIDTH), F32))
        out_specs.append(row(SGU_WIDTH))
    outs = pl.pallas_call(
        functools.partial(_sgu_kernel, n_chunks=tm // SGU_CHUNK),
        grid=(m // tm,),
        in_specs=[row(d), _const_spec((1, d)), _layer_spec(w_in_all, layer),
                  _const_spec((1, SGU_WIDTH)), _const_spec((1, SGU_WIDTH)),
                  _const_spec((SGU_GROUPS, SGU_CHUNK, SGU_CHUNK)),
                  _const_spec((SGU_CHUNK, SGU_GROUPS))],
        out_specs=out_specs,
        out_shape=out_shape,
        compiler_params=_cparams("parallel"),
        name="sgu_mix",
    )(h, g_mix.reshape(1, d), w_in_all, ln_g.reshape(1, -1), ln_b.reshape(1, -1), ws, bs)
    return outs if want_v else (outs[0], None)


def _rope_tables(pos):
    half = ATTN_DK // 2
    inv = ROPE_THETA ** (-jnp.arange(half, dtype=F32) / half)
    ang = pos.astype(F32)[:, None] * inv[None, :]
    cos, sin = jnp.cos(ang), jnp.sin(ang)
    cos_t = jnp.tile(cos, (1, 4))
    sin_t = jnp.tile(jnp.concatenate([-sin, sin], axis=1), (1, 2))
    return cos_t, sin_t


def _rope_head(x, cos, sin_signed, first_half):
    partner = jnp.where(first_half, pltpu.roll(x, LANE - ATTN_DK // 2, 1),
                        pltpu.roll(x, ATTN_DK // 2, 1))
    return x * cos + partner * sin_signed


def _qkv_kernel(h_ref, g_ref, w_ref, cos_ref, sin_ref, qb_ref, k_ref, v_ref, kb_ref, vb_ref, *,
                transposed_v):
    xn = _rms(h_ref[...], g_ref[...]).astype(BF16)
    qkv = _dot(xn, w_ref[...])
    cos = cos_ref[...]
    sin = sin_ref[...]
    lane = lax.broadcasted_iota(jnp.int32, cos.shape, 1)
    first_half = (lane % ATTN_DK) < (ATTN_DK // 2)
    qk_dim = ATTN_HEADS * ATTN_HD
    scale = ATTN_DK ** -0.5 * math.log2(math.e)
    rows = h_ref.shape[0]
    for hd in range(ATTN_HEADS):
        cols = slice(hd * ATTN_HD, (hd + 1) * ATTN_HD)
        q = _rope_head(qkv[:, cols], cos, sin, first_half)
        k = _rope_head(qkv[:, qk_dim + hd * ATTN_HD:qk_dim + (hd + 1) * ATTN_HD], cos, sin,
                       first_half)
        qb_ref[:, cols] = (q * scale).astype(BF16)
        k_ref[pl.ds(hd, rows, stride=ATTN_HEADS), :] = k
        kb_ref[:, cols] = k.astype(BF16)
        v_ref[pl.ds(hd, rows, stride=ATTN_HEADS), :] = qkv[:, 2 * qk_dim + hd * ATTN_DV:
                                                           2 * qk_dim + (hd + 1) * ATTN_DV]
    v = qkv[:, 2 * qk_dim:]
    vb_ref[...] = (v.T if transposed_v else v).astype(BF16)


def qkv_rope(h, g_mix, w_qkv_all, layer, cos_t, sin_t, *, transposed_v=False, tile=512):
    m, d = h.shape
    seq = cos_t.shape[0]
    tm = _row_tile(m, tile)
    row = lambda w: pl.BlockSpec((tm, w), lambda i: (i, 0))
    if seq % tm == 0:
        n_blk = seq // tm
        table = pl.BlockSpec((tm, LANE), lambda i: (i % n_blk, 0))
    else:
        assert tm % seq == 0, (tm, seq)
        cos_t = jnp.tile(cos_t, (tm // seq, 1))
        sin_t = jnp.tile(sin_t, (tm // seq, 1))
        table = _const_spec((tm, LANE))
    hw = ATTN_HEADS * ATTN_HD
    vb_spec, vb_shape = row(hw), jax.ShapeDtypeStruct((m, hw), BF16)
    head_rows = pl.BlockSpec((tm * ATTN_HEADS, ATTN_HD), lambda i: (i, 0))
    if transposed_v:
        assert seq % tm == 0
        n_blk = seq // tm
        vb_spec = pl.BlockSpec((None, hw, tm), lambda i: (i // n_blk, 0, i % n_blk))
        vb_shape = jax.ShapeDtypeStruct((m // seq, hw, seq), BF16)
    return pl.pallas_call(
        functools.partial(_qkv_kernel, transposed_v=transposed_v),
        grid=(m // tm,),
        in_specs=[row(d), _const_spec((1, d)), _layer_spec(w_qkv_all, layer), table, table],
        out_specs=[row(hw), head_rows, head_rows, row(hw), vb_spec],
        out_shape=[jax.ShapeDtypeStruct((m, hw), BF16),
                   jax.ShapeDtypeStruct((m * ATTN_HEADS, ATTN_HD), F32),
                   jax.ShapeDtypeStruct((m * ATTN_HEADS, ATTN_DV), F32),
                   jax.ShapeDtypeStruct((m, hw), BF16), vb_shape],
        compiler_params=_cparams("parallel"),
        name="qkv_rope",
    )(h, g_mix.reshape(1, d), w_qkv_all, cos_t, sin_t)


def _lambda_value(lam_ref, lam_init):
    lp = lam_ref[...]
    s1 = jnp.sum(lp[0:1] * lp[1:2], axis=-1, keepdims=True)
    s2 = jnp.sum(lp[2:3] * lp[3:4], axis=-1, keepdims=True)
    return jnp.exp(s1) - jnp.exp(s2) + lam_init


def _softmax_step_t(s, v, m_prev, l_prev, acc_prev):
    m_new = jnp.maximum(m_prev, jnp.max(s, axis=0, keepdims=True))
    alpha = jnp.exp2(m_prev - m_new)
    p = jnp.exp2(s - m_new)
    l_new = alpha * l_prev + jnp.sum(p, axis=0, keepdims=True)
    acc = alpha * acc_prev + _dot(v, p.astype(BF16))
    return m_new, l_new, acc


def _attn_prompt_kernel(lam_ref, g_ref, q_ref, k_ref, v_ref, o_ref, *, tile, lam_init):
    qi = pl.program_id(2)
    half = tile // 2
    q = q_ref[...]
    lane = lax.broadcasted_iota(jnp.int32, q.shape, 1)
    zero = jnp.zeros_like(q)
    q1 = jnp.where(lane < ATTN_DK, q, zero)
    q2 = jnp.where(lane < ATTN_DK, zero, q)

    def block(start, n_keys, q_lo, carry, masked):
        k = k_ref[pl.ds(start, n_keys), :]
        v = v_ref[:, pl.ds(start, n_keys)]
        s1 = _dot_nt(k, q1[q_lo:])
        s2 = _dot_nt(k, q2[q_lo:])
        if masked:
            kr = lax.broadcasted_iota(jnp.int32, s1.shape, 0)
            qc = lax.broadcasted_iota(jnp.int32, s1.shape, 1)
            s1 = jnp.where(kr <= qc, s1, -jnp.inf)
            s2 = jnp.where(kr <= qc, s2, -jnp.inf)
        c1, c2 = carry
        return _softmax_step_t(s1, v, *c1), _softmax_step_t(s2, v, *c2)

    init = (jnp.full((1, tile), -jnp.inf, F32), jnp.zeros((1, tile), F32),
            jnp.zeros((ATTN_DV, tile), F32))
    full = lambda ki, c: block(pl.multiple_of(ki * tile, tile), tile, 0, c, False)
    carry = lax.fori_loop(0, qi, full, (init, init))
    base = pl.multiple_of(qi * tile, tile)
    carry = block(base, half, 0, carry, True)
    tail = tuple(tuple(x[:, half:] for x in c) for c in carry)
    tail = block(base + half, half, half, tail, True)
    (_, l1, a1), (_, l2, a2) = tuple(
        tuple(jnp.concatenate([x[:, :half], y], axis=1) for x, y in zip(c, t)) for c, t in zip(carry, tail))
    lam = _lambda_value(lam_ref, lam_init)
    o = (a1 / l1 - lam * (a2 / l2)).T
    o = _rms(o, g_ref[...]) * (1.0 - lam_init)
    o_ref[...] = o.astype(BF16)


def attn_prompt(qb, kb, vb_t, lam_p, subln_g, *, batch, lam_init, tile=1024):
    m, hw = qb.shape
    seq = m // batch
    t = _row_tile(seq, tile)
    q3, k3 = (x.reshape(batch, seq, hw) for x in (qb, kb))
    q_spec = pl.BlockSpec((None, t, ATTN_HD), lambda b, h, qi: (b, qi, h))
    k_spec = pl.BlockSpec((None, seq, ATTN_HD), lambda b, h, qi: (b, 0, h))
    vt_spec = pl.BlockSpec((None, ATTN_DV, seq), lambda b, h, qi: (b, h, 0))
    const = lambda shape: pl.BlockSpec(shape, lambda b, h, qi: (0,) * len(shape))
    out = pl.pallas_call(
        functools.partial(_attn_prompt_kernel, tile=t, lam_init=lam_init),
        grid=(batch, ATTN_HEADS, seq // t),
        in_specs=[const((4, ATTN_DK)), const((1, ATTN_DV)), q_spec, k_spec, vt_spec],
        out_specs=q_spec,
        out_shape=jax.ShapeDtypeStruct((batch, seq, hw), BF16),
        compiler_params=_cparams("parallel", "parallel", "parallel"),
        name="attn_prompt",
    )(lam_p, subln_g.reshape(1, ATTN_DV), q3, k3, vb_t)
    return out.reshape(m, hw)


NEW_ROWS = 16


def _decode_step(lam_ref, g_ref, q_ref, kn_ref, vn_ref, k_refs, v_refs, o_ref, m_ref, l_ref, acc_ref, *,
                 first, last, n_q, lam_init):
    rows = 2 * n_q
    q_all = q_ref[...].astype(BF16)

    def update(s, pv_fn, fresh):
        m_prev = jnp.where(fresh, -jnp.inf, m_ref[...])
        m_new = jnp.maximum(m_prev, jnp.max(s, axis=1, keepdims=True))
        alpha = jnp.exp2(m_prev - m_new)
        p = jnp.exp2(s - m_new)
        l_ref[...] = jnp.where(fresh, 0.0, alpha * l_ref[...]) + jnp.sum(p, axis=1, keepdims=True)
        acc_ref[...] = jnp.where(fresh, 0.0, alpha * acc_ref[...]) + pv_fn(p)
        m_ref[...] = m_new

    n_cols = k_refs[0].shape[0]
    r = lax.broadcasted_iota(jnp.int32, (q_all.shape[0], n_cols), 0)
    c = lax.broadcasted_iota(jnp.int32, (q_all.shape[0], n_cols), 1)
    own_head = (c % ATTN_HEADS) == (r // rows)
    s = jnp.concatenate(
        [jnp.where(own_head, _dot_nt(q_all, k_ref[...].astype(BF16)), -jnp.inf) for k_ref in k_refs],
        axis=1)

    def pv_pages(p):
        acc = None
        for i, v_ref in enumerate(v_refs):
            d = _dot(p[:, i * n_cols:(i + 1) * n_cols].astype(BF16), v_ref[...].astype(BF16))
            acc = d if acc is None else acc + d
        return acc

    update(s, pv_pages, first)

    @pl.when(last)
    def _():
        kn = kn_ref[...]
        vn = vn_ref[...]
        head_cols = lambda x, h: x[:, h * ATTN_HD:(h + 1) * ATTN_HD]
        q_heads = [q_all[h * rows:(h + 1) * rows] for h in range(ATTN_HEADS)]
        s_new = jnp.concatenate([_dot_nt(q_heads[h], head_cols(kn, h)) for h in range(ATTN_HEADS)], axis=0)
        rn = lax.broadcasted_iota(jnp.int32, s_new.shape, 0)
        cn = lax.broadcasted_iota(jnp.int32, s_new.shape, 1)
        s_new = jnp.where(cn <= rn % n_q, s_new, -jnp.inf)

        def pv_new(p):
            return jnp.concatenate(
                [_dot(p[h * rows:(h + 1) * rows].astype(BF16), head_cols(vn, h)) for h in range(ATTN_HEADS)],
                axis=0)

        update(s_new, pv_new, False)
        lam = _lambda_value(lam_ref, lam_init)
        o_all = acc_ref[...] / l_ref[...]
        for h in range(ATTN_HEADS):
            r1 = h * rows
            o = o_all[r1:r1 + n_q] - lam * o_all[r1 + n_q:r1 + rows]
            o = _rms(o, g_ref[...]) * (1.0 - lam_init)
            o_ref[:, h * ATTN_DV:(h + 1) * ATTN_DV] = o.astype(BF16)


def _zero_decode_stats(m_ref, l_ref, acc_ref):
    m_ref[...] = jnp.zeros(m_ref.shape, F32)
    l_ref[...] = jnp.zeros(l_ref.shape, F32)
    acc_ref[...] = jnp.zeros(acc_ref.shape, F32)


def _attn_decode_kernel(pt_ref, lam_ref, g_ref, q_ref, kn_ref, vn_ref, *refs, pages, n_q, lam_init):
    o_ref, m_ref, l_ref, acc_ref = refs[2 * pages:]
    step = pl.program_id(1)
    pl.when(step == 0)(functools.partial(_zero_decode_stats, m_ref, l_ref, acc_ref))
    _decode_step(lam_ref, g_ref, q_ref, kn_ref, vn_ref, refs[:pages], refs[pages:2 * pages], o_ref,
                 m_ref, l_ref, acc_ref, first=step == 0, last=step == pl.num_programs(1) - 1,
                 n_q=n_q, lam_init=lam_init)


def _decode_operands(qb, kb_new, vb_new, cache_k, cache_v, page_table, layer):
    b, n_q, hw = qb.shape
    n_layers, n_pool, page, n_heads, hd = cache_k.shape
    assert n_q <= NEW_ROWS and (n_heads, hd) == (ATTN_HEADS, ATTN_HD)
    ck = cache_k.reshape(n_layers * n_pool, page * n_heads, hd)
    cv = cache_v.reshape(n_layers * n_pool, page * n_heads, hd)
    pt = (page_table.astype(jnp.int32) + layer * n_pool).reshape(-1)
    half = (jnp.arange(ATTN_HD) // ATTN_DK)[None, :] == jnp.arange(2)[:, None]
    q4 = qb.reshape(b, n_q, ATTN_HEADS, ATTN_HD).astype(F32)
    qrows = jnp.where(half[None, None, :, None, :], jnp.transpose(q4, (0, 2, 1, 3))[:, :, None], 0.0)
    qrows = qrows.reshape(b, ATTN_HEADS * 2 * n_q, ATTN_HD)
    pad = ((0, 0), (0, NEW_ROWS - n_q), (0, 0))
    return pt, qrows, jnp.pad(kb_new, pad), jnp.pad(vb_new, pad), ck, cv


def _decode_scratch(n_q):
    n_rows = ATTN_HEADS * 2 * n_q
    return [pltpu.VMEM((n_rows, 1), F32), pltpu.VMEM((n_rows, 1), F32), pltpu.VMEM((n_rows, ATTN_DV), F32)]


def _decode_split(steps, n_seq, n_pages):
    if steps % n_seq:
        return None
    steps_per_seq = steps // n_seq
    if n_pages % steps_per_seq:
        return None
    return steps_per_seq, n_pages // steps_per_seq


def attn_decode(qb, kb_new, vb_new, cache_k, cache_v, page_table, layer, lam_p, subln_g, *,
                lam_init, pages=8):
    b, n_q, hw = qb.shape
    n_pages = page_table.shape[1]
    assert n_pages % pages == 0
    pt, qrows, kn, vn, ck, cv = _decode_operands(qb, kb_new, vb_new, cache_k, cache_v, page_table, layer)
    n_rows = qrows.shape[1]

    def page_spec(i):
        return pl.BlockSpec((None,) + ck.shape[1:],
                            lambda bi, s, pt_ref: (pt_ref[bi * n_pages + s * pages + i], 0, 0))

    per_b = lambda r, w: pl.BlockSpec((None, r, w), lambda bi, s, pt_ref: (bi, 0, 0))
    const = lambda shape: pl.BlockSpec(shape, lambda bi, s, pt_ref: (0,) * len(shape))
    grid_spec = pltpu.PrefetchScalarGridSpec(
        num_scalar_prefetch=1,
        grid=(b, n_pages // pages),
        in_specs=[const((4, ATTN_DK)), const((1, ATTN_DV)), per_b(n_rows, ATTN_HD), per_b(NEW_ROWS, hw),
                  per_b(NEW_ROWS, hw)] + [page_spec(i) for i in range(pages)] * 2,
        out_specs=per_b(n_q, hw),
        scratch_shapes=_decode_scratch(n_q),
    )
    return pl.pallas_call(
        functools.partial(_attn_decode_kernel, pages=pages, n_q=n_q, lam_init=lam_init),
        grid_spec=grid_spec,
        out_shape=jax.ShapeDtypeStruct((b, n_q, hw), BF16),
        compiler_params=_cparams("parallel", "arbitrary"),
        name="attn_decode",
    )(pt, lam_p, subln_g.reshape(1, ATTN_DV), qrows, kn, vn, *([ck] * pages), *([cv] * pages))


def _ssm_in_kernel(h_ref, g_ref, w_ref, wdt_ref, z_ref, xbc_ref, dt_ref):
    xn = _rms(h_ref[...], g_ref[...]).astype(BF16)
    z_ref[...] = _dot(xn, w_ref[:, :SSM_INNER])
    xbc_ref[...] = _dot(xn, w_ref[:, SSM_INNER:SSM_INNER + SSM_CONV_DIM])
    dt_ref[...] = _dot(xn, wdt_ref[...])


def ssm_in_proj(h, g_mix, w_in_all, layer, w_dt, *, tile=512):
    m, d = h.shape
    tm = _row_tile(m, tile)
    row = lambda w: pl.BlockSpec((tm, w), lambda i: (i, 0))
    return pl.pallas_call(
        _ssm_in_kernel,
        grid=(m // tm,),
        in_specs=[row(d), _const_spec((1, d)), _layer_spec(w_in_all, layer), _const_spec((d, LANE))],
        out_specs=[row(SSM_INNER), row(SSM_CONV_DIM), row(LANE)],
        out_shape=[jax.ShapeDtypeStruct((m, SSM_INNER), F32),
                   jax.ShapeDtypeStruct((m, SSM_CONV_DIM), F32),
                   jax.ShapeDtypeStruct((m, LANE), F32)],
        compiler_params=_cparams("parallel"),
        name="ssm_in_proj",
    )(h, g_mix.reshape(1, d), w_in_all, w_dt)


def _split3(x):
    x1 = x.astype(BF16)
    r = x - x1.astype(F32)
    x2 = r.astype(BF16)
    x3 = (r - x2.astype(F32)).astype(BF16)
    return x1, x2, x3


def _spread_dot(x, sel):
    x1 = x.astype(BF16)
    x2 = (x - x1.astype(F32)).astype(BF16)
    return _dot(x1, sel) + _dot(x2, sel)


def _exact_dot_left(sel, x):
    x1, x2, x3 = _split3(x)
    return _dot(sel, x1) + _dot(sel, x2) + _dot(sel, x3)


N_SSD_IN = 12


def _ssd_body(z_ref, xbc_ref, dt_ref, cbuf_ref, h0_ref, cw_ref, cb_ref, dtb_ref, alog_ref,
              dskip_ref, ng_ref, spread_ref, y_ref, hf_ref, state_ref, tail_ref, *, valid,
              state_out_every_step=False):
    c = pl.program_id(1)
    q = xbc_ref.shape[0]

    @pl.when(c == 0)
    def _():
        state_ref[...] = h0_ref[...]
        tail_ref[...] = cbuf_ref[...]

    cur = xbc_ref[...]
    prev = tail_ref[...]
    sub = lax.broadcasted_iota(jnp.int32, prev.shape, 0)
    conv = cb_ref[...] + cur * cw_ref[SSM_CONV - 1:SSM_CONV, :]
    for tap in range(SSM_CONV - 1):
        shift = SSM_CONV - 1 - tap
        rolled = pltpu.roll(cur, shift, 0)
        head = jnp.where(sub < shift, pltpu.roll(prev, shift, 0), rolled[0:8])
        shifted = jnp.concatenate([head, rolled[8:]], axis=0)
        conv = conv + shifted * cw_ref[tap:tap + 1, :]
    tail_ref[...] = cur[q - 8:q]
    conv = conv * jax.nn.sigmoid(conv)
    xs = conv[:, :SSM_INNER]
    bm = conv[:, SSM_INNER:SSM_INNER + SSM_GN].astype(BF16)
    cm = conv[:, SSM_INNER + SSM_GN:].astype(BF16)

    row = lax.broadcasted_iota(jnp.int32, (q, q), 0)
    col = lax.broadcasted_iota(jnp.int32, (q, q), 1)
    tri = col <= row
    dt = jax.nn.softplus(dt_ref[...] + dtb_ref[...])
    if valid < q:
        dt = jnp.where(lax.broadcasted_iota(jnp.int32, dt.shape, 0) < valid, dt, 0.0)
    a = dt * (-jnp.exp(alog_ref[...]))
    acum = _exact_dot_left(tri.astype(BF16), a)
    acum_t = acum.T
    spread = spread_ref[...]
    dt_x = _spread_dot(dt, spread)
    grow_x = _spread_dot(jnp.exp(acum), spread)
    wst_x = _spread_dot(jnp.exp(acum[q - 1:q, :] - acum), spread)
    chunk_decay = jnp.exp(acum_t[:, q - 1:q])

    xd = xs * dt_x
    xd_b = xd.astype(BF16)
    xw_b = (xd * wst_x).astype(BF16)
    y_parts = []
    for g in range(SSM_GROUPS):
        ncols = slice(g * SSM_STATE, (g + 1) * SSM_STATE)
        cb = _dot_nt(cm[:, ncols], bm[:, ncols])
        heads = range(g * SSM_HPG, (g + 1) * SSM_HPG)
        s_g = jnp.concatenate([state_ref[h] for h in heads], axis=0)
        gcols = slice(g * SSM_HPG * SSM_HEAD_DIM, (g + 1) * SSM_HPG * SSM_HEAD_DIM)
        y_off = _dot_nt(cm[:, ncols], s_g.astype(BF16)) * grow_x[:, gcols]
        y_diag = []
        for h in heads:
            seg = acum[:, h:h + 1] - acum_t[h:h + 1, :]
            lmat = (cb * jnp.exp(jnp.where(tri, seg, -jnp.inf))).astype(BF16)
            y_diag.append(_dot(lmat, xd_b[:, h * SSM_HEAD_DIM:(h + 1) * SSM_HEAD_DIM]))
        y_parts.append(jnp.concatenate(y_diag, axis=1) + y_off)
        new_states = _dot_tn(xw_b[:, gcols], bm[:, ncols])
        for i, h in enumerate(heads):
            state_ref[h] = state_ref[h] * chunk_decay[h:h + 1, :] + new_states[i * SSM_HEAD_DIM:(i + 1) * SSM_HEAD_DIM, :]
    y = jnp.concatenate(y_parts, axis=1) + xs * dskip_ref[...]
    zf = z_ref[...]
    y = y * (zf * jax.nn.sigmoid(zf))
    gw = SSM_INNER // SSM_GROUPS
    out = []
    for g in range(SSM_GROUPS):
        yg = y[:, g * gw:(g + 1) * gw]
        out.append(_rms(yg, ng_ref[:, g * gw:(g + 1) * gw]))
    y_ref[...] = jnp.concatenate(out, axis=1).astype(BF16)

    if state_out_every_step:
        hf_ref[...] = state_ref[...]
    else:
        @pl.when(c == pl.num_programs(1) - 1)
        def _():
            hf_ref[...] = state_ref[...]


def _ssd_kernel(*refs, valid):
    _ssd_body(*refs, valid=valid)


def _ssd_decode_kernel(pt_ref, *refs, valid, pages, n_q, lam_init, steps_per_seq):
    ssd_in, rest = refs[:N_SSD_IN], refs[N_SSD_IN:]
    lam_ref, g_ref, q_ref, kn_ref, vn_ref = rest[:5]
    k_refs, v_refs = rest[5:5 + pages], rest[5 + pages:5 + 2 * pages]
    y_ref, hf_ref, o_ref, state_ref, tail_ref, m_ref, l_ref, acc_ref = rest[5 + 2 * pages:]
    t = pl.program_id(0) * pl.num_programs(1) + pl.program_id(1)
    pl.when(t == 0)(functools.partial(_zero_decode_stats, m_ref, l_ref, acc_ref))
    _ssd_body(*ssd_in, y_ref, hf_ref, state_ref, tail_ref, valid=valid, state_out_every_step=True)
    part = t % steps_per_seq
    _decode_step(lam_ref, g_ref, q_ref, kn_ref, vn_ref, k_refs, v_refs, o_ref, m_ref, l_ref, acc_ref,
                 first=part == 0, last=part == steps_per_seq - 1, n_q=n_q, lam_init=lam_init)


def ssd_scan(z, xbc, dt_raw, conv_buf, h0, conv_w, conv_b, dt_bias, a_log, d_skip, norm_g, *,
             batch, chunk_len, valid, decode=None):
    m = z.shape[0]
    seq = m // batch
    q = chunk_len
    assert seq % q == 0 and (valid == q or seq == q)
    nc = seq // q
    z3 = z.reshape(batch, seq, SSM_INNER)
    x3 = xbc.reshape(batch, seq, SSM_CONV_DIM)
    d3 = dt_raw.reshape(batch, seq, LANE)
    cbuf = jnp.pad(conv_buf, ((0, 0), (8 - (SSM_CONV - 1), 0), (0, 0)))
    spread = (jnp.arange(SSM_INNER)[None, :] // SSM_HEAD_DIM == jnp.arange(LANE)[:, None]).astype(BF16)
    lane_pad = lambda v: jnp.pad(v.reshape(1, -1), ((0, 0), (0, LANE - v.shape[-1])))
    chunk = lambda w: pl.BlockSpec((None, q, w), lambda b, c, *_: (b, c, 0))
    state_spec = pl.BlockSpec((None, SSM_HEADS, SSM_HEAD_DIM, SSM_STATE), lambda b, c, *_: (b, 0, 0, 0))
    const = lambda shape: pl.BlockSpec(shape, lambda b, c, *_: (0,) * len(shape))
    in_specs = [chunk(SSM_INNER), chunk(SSM_CONV_DIM), chunk(LANE),
                pl.BlockSpec((None, 8, SSM_CONV_DIM), lambda b, c, *_: (b, 0, 0)), state_spec,
                const((SSM_CONV, SSM_CONV_DIM)), const((1, SSM_CONV_DIM)), const((1, LANE)),
                const((1, LANE)), const((1, SSM_INNER)), const((1, SSM_INNER)),
                const((LANE, SSM_INNER))]
    assert len(in_specs) == N_SSD_IN
    operands = (z3, x3, d3, cbuf, h0, conv_w, conv_b.reshape(1, -1), lane_pad(dt_bias), lane_pad(a_log),
                jnp.repeat(d_skip, SSM_HEAD_DIM).reshape(1, -1), norm_g.reshape(1, -1), spread)
    out_specs = [chunk(SSM_INNER), state_spec]
    out_shape = [jax.ShapeDtypeStruct((batch, seq, SSM_INNER), BF16), jax.ShapeDtypeStruct(h0.shape, F32)]
    scratch = [pltpu.VMEM((SSM_HEADS, SSM_HEAD_DIM, SSM_STATE), F32), pltpu.VMEM((8, SSM_CONV_DIM), F32)]
    if decode is None:
        y, hf = pl.pallas_call(
            functools.partial(_ssd_kernel, valid=valid),
            grid=(batch, nc), in_specs=in_specs, out_specs=out_specs, out_shape=out_shape,
            scratch_shapes=scratch, compiler_params=_cparams("parallel", "arbitrary"), name="ssd_scan",
        )(*operands)
        return y.reshape(m, SSM_INNER), hf

    b_dec, n_q, hw = decode["qb"].shape
    n_pages = decode["page_table"].shape[1]
    steps_per_seq, pages = _decode_split(batch * nc, b_dec, n_pages)
    pt, qrows, kn, vn, ck, cv = _decode_operands(decode["qb"], decode["kb_new"], decode["vb_new"],
                                                 decode["cache_k"], decode["cache_v"],
                                                 decode["page_table"], decode["layer"])
    n_rows = qrows.shape[1]
    step_of = lambda b, c: b * nc + c

    def page_spec(i):
        def index(b, c, pt_ref):
            t = step_of(b, c)
            return (pt_ref[(t // steps_per_seq) * n_pages + (t % steps_per_seq) * pages + i], 0, 0)
        return pl.BlockSpec((None,) + ck.shape[1:], index)

    per_seq = lambda r, w: pl.BlockSpec((None, r, w), lambda b, c, pt_ref: (step_of(b, c) // steps_per_seq, 0, 0))
    dec_specs = [const((4, ATTN_DK)), const((1, ATTN_DV)), per_seq(n_rows, ATTN_HD), per_seq(NEW_ROWS, hw),
                 per_seq(NEW_ROWS, hw)] + [page_spec(i) for i in range(pages)] * 2
    grid_spec = pltpu.PrefetchScalarGridSpec(
        num_scalar_prefetch=1, grid=(batch, nc),
        in_specs=in_specs + dec_specs,
        out_specs=out_specs + [per_seq(n_q, hw)],
        scratch_shapes=scratch + _decode_scratch(n_q),
    )
    y, hf, o_dec = pl.pallas_call(
        functools.partial(_ssd_decode_kernel, valid=valid, pages=pages, n_q=n_q,
                          lam_init=decode["lam_init"], steps_per_seq=steps_per_seq),
        grid_spec=grid_spec,
        out_shape=out_shape + [jax.ShapeDtypeStruct((b_dec, n_q, hw), BF16)],
        compiler_params=_cparams("arbitrary", "arbitrary"),
        name="ssd_scan_decode",
    )(pt, *operands, decode["lam_p"], decode["subln_g"].reshape(1, ATTN_DV), qrows, kn, vn,
      *([ck] * pages), *([cv] * pages))
    return y.reshape(m, SSM_INNER), hf, o_dec


def _ssm_group(h, batch, seq, conv_buf, h0, g_mix, w_in_all, layer, w_dt, conv_w, conv_b, dt_bias, a_log,
               d_skip, norm_g, decode=None):
    d = h.shape[1]
    if seq % SSM_CHUNK == 0:
        padded = seq
        chunk_len = valid = SSM_CHUNK
        hin = h
    else:
        assert seq <= SSM_SHORT_CHUNK
        padded = chunk_len = SSM_SHORT_CHUNK
        valid = seq
        hin = jnp.pad(h.reshape(batch, seq, d), ((0, 0), (0, padded - seq), (0, 0))).reshape(-1, d)
    z, xbc, dt = ssm_in_proj(hin, g_mix, w_in_all, layer, w_dt)
    y, hf, *dec_out = ssd_scan(z, xbc, dt, conv_buf, h0, conv_w, conv_b, dt_bias, a_log, d_skip, norm_g,
                               batch=batch, chunk_len=chunk_len, valid=valid, decode=decode)
    xbc3 = xbc.reshape(batch, padded, SSM_CONV_DIM)[:, :seq]
    new_buf = jnp.concatenate([conv_buf, xbc3], axis=1)[:, seq:]
    y = y.reshape(batch, padded, SSM_INNER)[:, :seq].reshape(batch * seq, SSM_INNER)
    return (y, new_buf, hf, *dec_out)


def kernel(x_prompt, x_sample, cache_k, cache_v, state_ssm, state_conv, page_table, p_prompt, p_sample, norm_mix_g, norm_mlp_g, norm_ple_g, mlp_w_up, mlp_w_down, ple_w_gate, ple_w_proj, final_norm_g, sgu_w_in, sgu_ln_g, sgu_ln_b, sgu_w_s, sgu_b_s, sgu_w_out, attn_w_qkv, attn_lambda, attn_subln_g, attn_w_out, ssm_w_in, ssm_conv_w, ssm_conv_b, ssm_dt_bias, ssm_a_log, ssm_d, ssm_norm_g, ssm_w_out):
    b_p, l_p, d = x_prompt.shape
    b_s, l_s, _ = x_sample.shape
    depth = norm_mix_g.shape[0]
    past_len = page_table.shape[1] * cache_k.shape[2]
    m_s = b_s * l_s
    assert m_s % SGU_CHUNK == 0 and SGU_CHUNK % l_s == 0 and l_p % SGU_CHUNK == 0
    bf = lambda w: w.astype(BF16)

    hp = x_prompt.reshape(b_p * l_p, d)
    hs = x_sample.reshape(m_s, d)
    rope_p = _rope_tables(jnp.arange(l_p, dtype=jnp.int32))
    rope_s = _rope_tables(past_len + jnp.arange(l_s, dtype=jnp.int32))

    w_up, w_down, w_gate, w_proj = bf(mlp_w_up), bf(mlp_w_down), bf(ple_w_gate), bf(ple_w_proj)
    w_sgu_in, w_sgu_out = bf(sgu_w_in), bf(sgu_w_out)
    w_qkv, w_attn_out = bf(attn_w_qkv), bf(attn_w_out)
    w_ssm_in, w_ssm_out = bf(ssm_w_in), bf(ssm_w_out)
    pp = p_prompt.reshape(depth, b_p * l_p, D_PLE)
    ps = p_sample.reshape(depth, m_s, D_PLE)

    k_p, v_p, k_s, v_s = [], [], [], []
    ssm_p, conv_p, ssm_s, conv_s = [], [], [], []
    sgu_s = []

    def channel(h, mix, p_all, i, w_mix, j, tile):
        return channel_update(h, mix, p_all, i, w_mix, j, norm_mlp_g[i], w_up, w_down, norm_ple_g[i],
                              w_gate, w_proj, final_norm_g, final_norm=i == depth - 1, tile=tile)

    def decode_rides_along(i):
        if i + 1 >= depth or (i + 1) % N_MIXERS != 2 or l_p % SSM_CHUNK:
            return False
        return _decode_split(b_p * (l_p // SSM_CHUNK), b_s, page_table.shape[1]) is not None

    pending = None
    for i in range(depth):
        kind, j = i % N_MIXERS, i // N_MIXERS
        g_mix = norm_mix_g[i]
        if kind == 0:
            ws_p, bs_p = _sgu_spatial(sgu_w_s[j], sgu_b_s[j], l_p)
            ws_s, bs_s = _sgu_spatial(sgu_w_s[j], sgu_b_s[j], l_s)
            mix_p, _ = sgu_mix(hp, g_mix, w_sgu_in, j, sgu_ln_g[j], sgu_ln_b[j], ws_p, bs_p, want_v=False)
            mix_s, v_rows = sgu_mix(hs, g_mix, w_sgu_in, j, sgu_ln_g[j], sgu_ln_b[j], ws_s, bs_s,
                                    want_v=True, tile=SGU_CHUNK)
            sgu_s.append(v_rows.reshape(b_s, l_s, SGU_WIDTH))
            w_mix = w_sgu_out
        elif kind == 1:
            lam_init = 0.8 - 0.6 * math.exp(-0.3 * i)
            qb, k, v, kb, vb = qkv_rope(hp, g_mix, w_qkv, j, *rope_p, transposed_v=True)
            mix_p = attn_prompt(qb, kb, vb, attn_lambda[j], attn_subln_g[j], batch=b_p, lam_init=lam_init)
            k_p.append(k.reshape(b_p, l_p, ATTN_HEADS, ATTN_HD))
            v_p.append(v.reshape(b_p, l_p, ATTN_HEADS, ATTN_DV))
            qb, k, v, kb, vb = qkv_rope(hs, g_mix, w_qkv, j, *rope_s, tile=SGU_CHUNK)
            k_s.append(k.reshape(b_s, l_s, ATTN_HEADS, ATTN_HD))
            v_s.append(v.reshape(b_s, l_s, ATTN_HEADS, ATTN_DV))
            seq3 = lambda x: x.reshape(b_s, l_s, -1)
            dec = dict(qb=seq3(qb), kb_new=seq3(kb), vb_new=seq3(vb), cache_k=cache_k, cache_v=cache_v,
                       page_table=page_table, layer=j, lam_p=attn_lambda[j], subln_g=attn_subln_g[j],
                       lam_init=lam_init)
            w_mix = w_attn_out
            if decode_rides_along(i):
                pending = (dec, (ps, i, w_mix, j))
                mix_s = None
            else:
                mix_s = attn_decode(**dec).reshape(m_s, -1)
        else:
            w_dt = jnp.pad(w_ssm_in[j][:, SSM_INNER + SSM_CONV_DIM:], ((0, 0), (0, LANE - SSM_HEADS)))
            shared = (g_mix, w_ssm_in, j, w_dt, ssm_conv_w[j], ssm_conv_b[j], ssm_dt_bias[j],
                      ssm_a_log[j], ssm_d[j], ssm_norm_g[j])
            zero_buf = jnp.zeros((b_p, SSM_CONV - 1, SSM_CONV_DIM), F32)
            zero_h = jnp.zeros((b_p, SSM_HEADS, SSM_HEAD_DIM, SSM_STATE), F32)
            if pending is None:
                mix_p, cb_p, hf_p = _ssm_group(hp, b_p, l_p, zero_buf, zero_h, *shared)
            else:
                dec, prev_channel = pending
                pending = None
                mix_p, cb_p, hf_p, o_dec = _ssm_group(hp, b_p, l_p, zero_buf, zero_h, *shared, decode=dec)
                hs = channel(hs, o_dec.reshape(m_s, -1), *prev_channel, SGU_CHUNK)
            mix_s, cb_s, hf_s = _ssm_group(hs, b_s, l_s, state_conv[j], state_ssm[j], *shared)
            ssm_p.append(hf_p)
            conv_p.append(cb_p)
            ssm_s.append(hf_s)
            conv_s.append(cb_s)
            w_mix = w_ssm_out
        hp = channel(hp, mix_p, pp, i, w_mix, j, 512)
        if mix_s is not None:
            hs = channel(hs, mix_s, ps, i, w_mix, j, SGU_CHUNK)

    return (hp.reshape(b_p, l_p, d), hs.reshape(b_s, l_s, d),
            jnp.stack(k_p), jnp.stack(v_p), jnp.stack(k_s), jnp.stack(v_s),
            jnp.stack(ssm_p), jnp.stack(conv_p), jnp.stack(ssm_s), jnp.stack(conv_s),
            jnp.stack(sgu_s))
```

```python
import functools
import math

import jax
import jax.numpy as jnp
from jax import lax
from jax.experimental import pallas as pl
from jax.experimental.pallas import tpu as pltpu

F32 = jnp.float32
BF16 = jnp.bfloat16

EPS = 1e-6
D_MODEL = 1024
D_FF = 4 * D_MODEL
D_PLE = 256
N_MIXERS = 3

SGU_CHUNK = 128
SGU_WIDTH = 2 * D_MODEL
SGU_GROUPS = 8
SGU_GDIM = SGU_WIDTH // SGU_GROUPS

ATTN_HEADS = 8
ATTN_DK = 64
ATTN_DV = 128
ATTN_HD = 2 * ATTN_DK
ROPE_THETA = 10000.0

SSM_INNER = 2 * D_MODEL
SSM_HEAD_DIM = 64
SSM_HEADS = SSM_INNER // SSM_HEAD_DIM
SSM_GROUPS = 4
SSM_STATE = 128
SSM_CONV = 4
SSM_CHUNK = 128
SSM_SHORT_CHUNK = 16
SSM_GN = SSM_GROUPS * SSM_STATE
SSM_CONV_DIM = SSM_INNER + 2 * SSM_GN
SSM_HPG = SSM_HEADS // SSM_GROUPS

VMEM_LIMIT_BYTES = 56 * 1024 * 1024
LANE = 128


def _cparams(*sem):
    return pltpu.CompilerParams(dimension_semantics=sem, vmem_limit_bytes=VMEM_LIMIT_BYTES)


def _const_spec(shape):
    zeros = (0,) * len(shape)
    return pl.BlockSpec(shape, lambda *_: zeros, pipeline_mode=pl.Buffered(1))


def _layer_spec(stack, layer):
    tail = tuple(stack.shape[1:])
    zeros = (0,) * len(tail)
    return pl.BlockSpec((None,) + tail, lambda *_: (layer,) + zeros, pipeline_mode=pl.Buffered(1))


def _rms(x, g):
    ms = jnp.mean(x * x, axis=-1, keepdims=True)
    return (x * lax.rsqrt(ms + EPS)) * g


def _dot(a, b):
    return jnp.dot(a, b, preferred_element_type=F32)


def _dot_nt(a, b):
    return lax.dot_general(a, b, (((1,), (1,)), ((), ())), preferred_element_type=F32)


def _dot_tn(a, b):
    return lax.dot_general(a, b, (((0,), (0,)), ((), ())), preferred_element_type=F32)


def _row_tile(m, want):
    t = min(m, want)
    assert m % t == 0, (m, t)
    return t


def _channel_kernel(h_ref, mix_ref, p_ref, wmix_ref, gmlp_ref, wup_ref, wdown_ref, gple_ref,
                    wgate_ref, wproj_ref, gfin_ref, out_ref, *, final_norm):
    h = h_ref[...] + _dot(mix_ref[...], wmix_ref[...])
    xn = _rms(h, gmlp_ref[...]).astype(BF16)
    a = jnp.square(jnp.maximum(_dot(xn, wup_ref[...]), 0.0)).astype(BF16)
    h = h + _dot(a, wdown_ref[...])
    xn = _rms(h, gple_ref[...]).astype(BF16)
    gate = jax.nn.sigmoid(_dot(xn, wgate_ref[...]))
    h = h + gate * _dot(p_ref[...].astype(BF16), wproj_ref[...])
    if final_norm:
        h = _rms(h, gfin_ref[...])
    out_ref[...] = h


def channel_update(h, mix, p_all, layer, w_mix_all, mix_layer, g_mlp, w_up_all, w_down_all, g_ple,
                   w_gate_all, w_proj_all, g_fin, *, final_norm, tile=512):
    m, d = h.shape
    dm = mix.shape[1]
    tm = _row_tile(m, tile)
    row = lambda w: pl.BlockSpec((tm, w), lambda i: (i, 0))
    return pl.pallas_call(
        functools.partial(_channel_kernel, final_norm=final_norm),
        grid=(m // tm,),
        in_specs=[row(d), row(dm), pl.BlockSpec((None, tm, D_PLE), lambda i: (layer, i, 0)),
                  _layer_spec(w_mix_all, mix_layer), _const_spec((1, d)),
                  _layer_spec(w_up_all, layer), _layer_spec(w_down_all, layer), _const_spec((1, d)),
                  _layer_spec(w_gate_all, layer), _layer_spec(w_proj_all, layer), _const_spec((1, d))],
        out_specs=row(d),
        out_shape=jax.ShapeDtypeStruct((m, d), F32),
        compiler_params=_cparams("parallel"),
        name="channel_update",
    )(h, mix, p_all, w_mix_all, g_mlp.reshape(1, d), w_up_all, w_down_all, g_ple.reshape(1, d),
      w_gate_all, w_proj_all, g_fin.reshape(1, d))


def _sgu_kernel(h_ref, g_ref, win_ref, lng_ref, lnb_ref, ws_ref, bs_ref, mix_ref, *v_out,
                n_chunks):
    xn = _rms(h_ref[...], g_ref[...]).astype(BF16)
    v = jax.nn.gelu(_dot(xn, win_ref[:, SGU_WIDTH:]))
    u = jax.nn.gelu(_dot(xn, win_ref[:, :SGU_WIDTH]))
    mu = jnp.mean(v, axis=-1, keepdims=True)
    vc = v - mu
    v = (vc * lax.rsqrt(jnp.mean(vc * vc, axis=-1, keepdims=True) + EPS)) * lng_ref[...] + lnb_ref[...]
    if v_out:
        v_out[0][...] = v
    vb = v.astype(BF16)
    bs = bs_ref[...]
    for c in range(n_chunks):
        rows = slice(c * SGU_CHUNK, (c + 1) * SGU_CHUNK)
        for g in range(SGU_GROUPS):
            cols = slice(g * SGU_GDIM, (g + 1) * SGU_GDIM)
            s = _dot(ws_ref[g], vb[rows, cols]) + bs[:, g:g + 1]
            mix_ref[rows, cols] = (u[rows, cols] * s).astype(BF16)


def _sgu_spatial(w_s, b_s, seq):
    q = min(SGU_CHUNK, seq)
    tri = jnp.tril(jnp.ones((q, q), dtype=bool))
    ws = jnp.where(tri, w_s[:, :q, :q], 0)
    bs = b_s[:, :q]
    rep = SGU_CHUNK // q
    if rep > 1:
        eye = jnp.eye(rep, dtype=ws.dtype)
        ws = jnp.einsum("ab,gts->gatbs", eye, ws).reshape(SGU_GROUPS, SGU_CHUNK, SGU_CHUNK)
        bs = jnp.tile(bs, (1, rep))
    return ws.astype(BF16), bs.T


def sgu_mix(h, g_mix, w_in_all, layer, ln_g, ln_b, ws, bs, *, want_v, tile=256):
    m, d = h.shape
    tm = _row_tile(m, tile)
    row = lambda w: pl.BlockSpec((tm, w), lambda i: (i, 0))
    out_shape = [jax.ShapeDtypeStruct((m, SGU_WIDTH), BF16)]
    out_specs = [row(SGU_WIDTH)]
    if want_v:
        out_shape.append(jax.ShapeDtypeStruct((m, SGU_WIDTH), F32))
        out_specs.append(row(SGU_WIDTH))
    outs = pl.pallas_call(
        functools.partial(_sgu_kernel, n_chunks=tm // SGU_CHUNK),
        grid=(m // tm,),
        in_specs=[row(d), _const_spec((1, d)), _layer_spec(w_in_all, layer),
                  _const_spec((1, SGU_WIDTH)), _const_spec((1, SGU_WIDTH)),
                  _const_spec((SGU_GROUPS, SGU_CHUNK, SGU_CHUNK)),
                  _const_spec((SGU_CHUNK, SGU_GROUPS))],
        out_specs=out_specs,
        out_shape=out_shape,
        compiler_params=_cparams("parallel"),
        name="sgu_mix",
    )(h, g_mix.reshape(1, d), w_in_all, ln_g.reshape(1, -1), ln_b.reshape(1, -1), ws, bs)
    return outs if want_v else (outs[0], None)


def _rope_tables(pos):
    half = ATTN_DK // 2
    inv = ROPE_THETA ** (-jnp.arange(half, dtype=F32) / half)
    ang = pos.astype(F32)[:, None] * inv[None, :]
    cos, sin = jnp.cos(ang), jnp.sin(ang)
    cos_t = jnp.tile(cos, (1, 4))
    sin_t = jnp.tile(jnp.concatenate([-sin, sin], axis=1), (1, 2))
    return cos_t, sin_t


def _rope_head(x, cos, sin_signed, first_half):
    partner = jnp.where(first_half, pltpu.roll(x, LANE - ATTN_DK // 2, 1),
                        pltpu.roll(x, ATTN_DK // 2, 1))
    return x * cos + partner * sin_signed


def _qkv_kernel(h_ref, g_ref, w_ref, cos_ref, sin_ref, qb_ref, k_ref, v_ref, kb_ref, vb_ref, *,
                transposed_v):
    xn = _rms(h_ref[...], g_ref[...]).astype(BF16)
    qkv = _dot(xn, w_ref[...])
    cos = cos_ref[...]
    sin = sin_ref[...]
    lane = lax.broadcasted_iota(jnp.int32, cos.shape, 1)
    first_half = (lane % ATTN_DK) < (ATTN_DK // 2)
    qk_dim = ATTN_HEADS * ATTN_HD
    scale = ATTN_DK ** -0.5 * math.log2(math.e)
    rows = h_ref.shape[0]
    for hd in range(ATTN_HEADS):
        cols = slice(hd * ATTN_HD, (hd + 1) * ATTN_HD)
        q = _rope_head(qkv[:, cols], cos, sin, first_half)
        k = _rope_head(qkv[:, qk_dim + hd * ATTN_HD:qk_dim + (hd + 1) * ATTN_HD], cos, sin,
                       first_half)
        qb_ref[:, cols] = (q * scale).astype(BF16)
        k_ref[pl.ds(hd, rows, stride=ATTN_HEADS), :] = k
        kb_ref[:, cols] = k.astype(BF16)
        v_ref[pl.ds(hd, rows, stride=ATTN_HEADS), :] = qkv[:, 2 * qk_dim + hd * ATTN_DV:
                                                           2 * qk_dim + (hd + 1) * ATTN_DV]
    v = qkv[:, 2 * qk_dim:]
    vb_ref[...] = (v.T if transposed_v else v).astype(BF16)


def qkv_rope(h, g_mix, w_qkv_all, layer, cos_t, sin_t, *, transposed_v=False, tile=512):
    m, d = h.shape
    seq = cos_t.shape[0]
    tm = _row_tile(m, tile)
    row = lambda w: pl.BlockSpec((tm, w), lambda i: (i, 0))
    if seq % tm == 0:
        n_blk = seq // tm
        table = pl.BlockSpec((tm, LANE), lambda i: (i % n_blk, 0))
    else:
        assert tm % seq == 0, (tm, seq)
        cos_t = jnp.tile(cos_t, (tm // seq, 1))
        sin_t = jnp.tile(sin_t, (tm // seq, 1))
        table = _const_spec((tm, LANE))
    hw = ATTN_HEADS * ATTN_HD
    vb_spec, vb_shape = row(hw), jax.ShapeDtypeStruct((m, hw), BF16)
    head_rows = pl.BlockSpec((tm * ATTN_HEADS, ATTN_HD), lambda i: (i, 0))
    if transposed_v:
        assert seq % tm == 0
        n_blk = seq // tm
        vb_spec = pl.BlockSpec((None, hw, tm), lambda i: (i // n_blk, 0, i % n_blk))
        vb_shape = jax.ShapeDtypeStruct((m // seq, hw, seq), BF16)
    return pl.pallas_call(
        functools.partial(_qkv_kernel, transposed_v=transposed_v),
        grid=(m // tm,),
        in_specs=[row(d), _const_spec((1, d)), _layer_spec(w_qkv_all, layer), table, table],
        out_specs=[row(hw), head_rows, head_rows, row(hw), vb_spec],
        out_shape=[jax.ShapeDtypeStruct((m, hw), BF16),
                   jax.ShapeDtypeStruct((m * ATTN_HEADS, ATTN_HD), F32),
                   jax.ShapeDtypeStruct((m * ATTN_HEADS, ATTN_DV), F32),
                   jax.ShapeDtypeStruct((m, hw), BF16), vb_shape],
        compiler_params=_cparams("parallel"),
        name="qkv_rope",
    )(h, g_mix.reshape(1, d), w_qkv_all, cos_t, sin_t)


def _lambda_value(lam_ref, lam_init):
    lp = lam_ref[...]
    s1 = jnp.sum(lp[0:1] * lp[1:2], axis=-1, keepdims=True)
    s2 = jnp.sum(lp[2:3] * lp[3:4], axis=-1, keepdims=True)
    return jnp.exp(s1) - jnp.exp(s2) + lam_init


def _softmax_step_t(s, v, m_prev, l_prev, acc_prev):
    m_new = jnp.maximum(m_prev, jnp.max(s, axis=0, keepdims=True))
    alpha = jnp.exp2(m_prev - m_new)
    p = jnp.exp2(s - m_new)
    l_new = alpha * l_prev + jnp.sum(p, axis=0, keepdims=True)
    acc = alpha * acc_prev + _dot(v, p.astype(BF16))
    return m_new, l_new, acc


def _attn_prompt_kernel(lam_ref, g_ref, q_ref, k_ref, v_ref, o_ref, *, tile, n_tiles, lam_init):
    qi = pl.program_id(2)
    half = tile // 2
    q = q_ref[...]
    lane = lax.broadcasted_iota(jnp.int32, q.shape, 1)
    zero = jnp.zeros_like(q)
    q1 = jnp.where(lane < ATTN_DK, q, zero)
    q2 = jnp.where(lane < ATTN_DK, zero, q)

    def block(start, n_keys, q_lo, carry, masked):
        k = k_ref[pl.ds(start, n_keys), :]
        v = v_ref[:, pl.ds(start, n_keys)]
        s1 = _dot_nt(k, q1[q_lo:])
        s2 = _dot_nt(k, q2[q_lo:])
        if masked:
            kr = lax.broadcasted_iota(jnp.int32, s1.shape, 0)
            qc = lax.broadcasted_iota(jnp.int32, s1.shape, 1)
            s1 = jnp.where(kr <= qc, s1, -jnp.inf)
            s2 = jnp.where(kr <= qc, s2, -jnp.inf)
        c1, c2 = carry
        return _softmax_step_t(s1, v, *c1), _softmax_step_t(s2, v, *c2)

    init = (jnp.full((1, tile), -jnp.inf, F32), jnp.zeros((1, tile), F32),
            jnp.zeros((ATTN_DV, tile), F32))
    lam = _lambda_value(lam_ref, lam_init)

    def tile_at(n_full):
        carry = (init, init)
        for ki in range(n_full):
            carry = block(ki * tile, tile, 0, carry, False)
        base = n_full * tile
        carry = block(base, half, 0, carry, True)
        tail = tuple(tuple(x[:, half:] for x in c) for c in carry)
        tail = block(base + half, half, half, tail, True)
        (_, l1, a1), (_, l2, a2) = tuple(
            tuple(jnp.concatenate([x[:, :half], y], axis=1) for x, y in zip(c, t)) for c, t in zip(carry, tail))
        o = (a1 / l1 - lam * (a2 / l2)).T
        o = _rms(o, g_ref[...]) * (1.0 - lam_init)
        o_ref[...] = o.astype(BF16)

    for n_full in range(n_tiles):
        pl.when(qi == n_full)(functools.partial(tile_at, n_full))


def attn_prompt(qb, kb, vb_t, lam_p, subln_g, *, batch, lam_init, tile=1024):
    m, hw = qb.shape
    seq = m // batch
    t = _row_tile(seq, tile)
    q3, k3 = (x.reshape(batch, seq, hw) for x in (qb, kb))
    q_spec = pl.BlockSpec((None, t, ATTN_HD), lambda b, h, qi: (b, qi, h))
    k_spec = pl.BlockSpec((None, seq, ATTN_HD), lambda b, h, qi: (b, 0, h))
    vt_spec = pl.BlockSpec((None, ATTN_DV, seq), lambda b, h, qi: (b, h, 0))
    const = lambda shape: pl.BlockSpec(shape, lambda b, h, qi: (0,) * len(shape))
    out = pl.pallas_call(
        functools.partial(_attn_prompt_kernel, tile=t, n_tiles=seq // t, lam_init=lam_init),
        grid=(batch, ATTN_HEADS, seq // t),
        in_specs=[const((4, ATTN_DK)), const((1, ATTN_DV)), q_spec, k_spec, vt_spec],
        out_specs=q_spec,
        out_shape=jax.ShapeDtypeStruct((batch, seq, hw), BF16),
        compiler_params=_cparams("parallel", "parallel", "parallel"),
        name="attn_prompt",
    )(lam_p, subln_g.reshape(1, ATTN_DV), q3, k3, vb_t)
    return out.reshape(m, hw)


NEW_ROWS = 16


def _decode_step(lam_ref, g_ref, q_ref, kn_ref, vn_ref, k_refs, v_refs, o_ref, m_ref, l_ref, acc_ref, *,
                 first, last, n_q, lam_init):
    rows = 2 * n_q
    q_all = q_ref[...].astype(BF16)

    def update(s, pv_fn, fresh):
        m_prev = jnp.where(fresh, -jnp.inf, m_ref[...])
        m_new = jnp.maximum(m_prev, jnp.max(s, axis=1, keepdims=True))
        alpha = jnp.exp2(m_prev - m_new)
        p = jnp.exp2(s - m_new)
        l_ref[...] = jnp.where(fresh, 0.0, alpha * l_ref[...]) + jnp.sum(p, axis=1, keepdims=True)
        acc_ref[...] = jnp.where(fresh, 0.0, alpha * acc_ref[...]) + pv_fn(p)
        m_ref[...] = m_new

    n_cols = k_refs[0].shape[0]
    r = lax.broadcasted_iota(jnp.int32, (q_all.shape[0], n_cols), 0)
    c = lax.broadcasted_iota(jnp.int32, (q_all.shape[0], n_cols), 1)
    own_head = (c % ATTN_HEADS) == (r // rows)
    s = jnp.concatenate(
        [jnp.where(own_head, _dot_nt(q_all, k_ref[...].astype(BF16)), -jnp.inf) for k_ref in k_refs],
        axis=1)

    def pv_pages(p):
        acc = None
        for i, v_ref in enumerate(v_refs):
            d = _dot(p[:, i * n_cols:(i + 1) * n_cols].astype(BF16), v_ref[...].astype(BF16))
            acc = d if acc is None else acc + d
        return acc

    update(s, pv_pages, first)

    @pl.when(last)
    def _():
        kn = kn_ref[...]
        vn = vn_ref[...]
        head_cols = lambda x, h: x[:, h * ATTN_HD:(h + 1) * ATTN_HD]
        q_heads = [q_all[h * rows:(h + 1) * rows] for h in range(ATTN_HEADS)]
        s_new = jnp.concatenate([_dot_nt(q_heads[h], head_cols(kn, h)) for h in range(ATTN_HEADS)], axis=0)
        rn = lax.broadcasted_iota(jnp.int32, s_new.shape, 0)
        cn = lax.broadcasted_iota(jnp.int32, s_new.shape, 1)
        s_new = jnp.where(cn <= rn % n_q, s_new, -jnp.inf)

        def pv_new(p):
            return jnp.concatenate(
                [_dot(p[h * rows:(h + 1) * rows].astype(BF16), head_cols(vn, h)) for h in range(ATTN_HEADS)],
                axis=0)

        update(s_new, pv_new, False)
        lam = _lambda_value(lam_ref, lam_init)
        o_all = acc_ref[...] / l_ref[...]
        for h in range(ATTN_HEADS):
            r1 = h * rows
            o = o_all[r1:r1 + n_q] - lam * o_all[r1 + n_q:r1 + rows]
            o = _rms(o, g_ref[...]) * (1.0 - lam_init)
            o_ref[:, h * ATTN_DV:(h + 1) * ATTN_DV] = o.astype(BF16)


def _zero_decode_stats(m_ref, l_ref, acc_ref):
    m_ref[...] = jnp.zeros(m_ref.shape, F32)
    l_ref[...] = jnp.zeros(l_ref.shape, F32)
    acc_ref[...] = jnp.zeros(acc_ref.shape, F32)


def _attn_decode_kernel(pt_ref, lam_ref, g_ref, q_ref, kn_ref, vn_ref, *refs, pages, n_q, lam_init):
    o_ref, m_ref, l_ref, acc_ref = refs[2 * pages:]
    step = pl.program_id(1)
    pl.when(step == 0)(functools.partial(_zero_decode_stats, m_ref, l_ref, acc_ref))
    _decode_step(lam_ref, g_ref, q_ref, kn_ref, vn_ref, refs[:pages], refs[pages:2 * pages], o_ref,
                 m_ref, l_ref, acc_ref, first=step == 0, last=step == pl.num_programs(1) - 1,
                 n_q=n_q, lam_init=lam_init)


def _decode_operands(qb, kb_new, vb_new, cache_k, cache_v, page_table, layer):
    b, n_q, hw = qb.shape
    n_layers, n_pool, page, n_heads, hd = cache_k.shape
    assert n_q <= NEW_ROWS and (n_heads, hd) == (ATTN_HEADS, ATTN_HD)
    ck = cache_k.reshape(n_layers * n_pool, page * n_heads, hd)
    cv = cache_v.reshape(n_layers * n_pool, page * n_heads, hd)
    pt = (page_table.astype(jnp.int32) + layer * n_pool).reshape(-1)
    half = (jnp.arange(ATTN_HD) // ATTN_DK)[None, :] == jnp.arange(2)[:, None]
    q4 = qb.reshape(b, n_q, ATTN_HEADS, ATTN_HD).astype(F32)
    qrows = jnp.where(half[None, None, :, None, :], jnp.transpose(q4, (0, 2, 1, 3))[:, :, None], 0.0)
    qrows = qrows.reshape(b, ATTN_HEADS * 2 * n_q, ATTN_HD)
    pad = ((0, 0), (0, NEW_ROWS - n_q), (0, 0))
    return pt, qrows, jnp.pad(kb_new, pad), jnp.pad(vb_new, pad), ck, cv


def _decode_scratch(n_q):
    n_rows = ATTN_HEADS * 2 * n_q
    return [pltpu.VMEM((n_rows, 1), F32), pltpu.VMEM((n_rows, 1), F32), pltpu.VMEM((n_rows, ATTN_DV), F32)]


def _decode_split(steps, n_seq, n_pages):
    if steps % n_seq:
        return None
    steps_per_seq = steps // n_seq
    if n_pages % steps_per_seq:
        return None
    return steps_per_seq, n_pages // steps_per_seq


def attn_decode(qb, kb_new, vb_new, cache_k, cache_v, page_table, layer, lam_p, subln_g, *,
                lam_init, pages=8):
    b, n_q, hw = qb.shape
    n_pages = page_table.shape[1]
    assert n_pages % pages == 0
    pt, qrows, kn, vn, ck, cv = _decode_operands(qb, kb_new, vb_new, cache_k, cache_v, page_table, layer)
    n_rows = qrows.shape[1]

    def page_spec(i):
        return pl.BlockSpec((None,) + ck.shape[1:],
                            lambda bi, s, pt_ref: (pt_ref[bi * n_pages + s * pages + i], 0, 0))

    per_b = lambda r, w: pl.BlockSpec((None, r, w), lambda bi, s, pt_ref: (bi, 0, 0))
    const = lambda shape: pl.BlockSpec(shape, lambda bi, s, pt_ref: (0,) * len(shape))
    grid_spec = pltpu.PrefetchScalarGridSpec(
        num_scalar_prefetch=1,
        grid=(b, n_pages // pages),
        in_specs=[const((4, ATTN_DK)), const((1, ATTN_DV)), per_b(n_rows, ATTN_HD), per_b(NEW_ROWS, hw),
                  per_b(NEW_ROWS, hw)] + [page_spec(i) for i in range(pages)] * 2,
        out_specs=per_b(n_q, hw),
        scratch_shapes=_decode_scratch(n_q),
    )
    return pl.pallas_call(
        functools.partial(_attn_decode_kernel, pages=pages, n_q=n_q, lam_init=lam_init),
        grid_spec=grid_spec,
        out_shape=jax.ShapeDtypeStruct((b, n_q, hw), BF16),
        compiler_params=_cparams("parallel", "arbitrary"),
        name="attn_decode",
    )(pt, lam_p, subln_g.reshape(1, ATTN_DV), qrows, kn, vn, *([ck] * pages), *([cv] * pages))


def _ssm_in_kernel(h_ref, g_ref, w_ref, wdt_ref, z_ref, xbc_ref, dt_ref):
    xn = _rms(h_ref[...], g_ref[...]).astype(BF16)
    z_ref[...] = _dot(xn, w_ref[:, :SSM_INNER])
    xbc_ref[...] = _dot(xn, w_ref[:, SSM_INNER:SSM_INNER + SSM_CONV_DIM])
    dt_ref[...] = _dot(xn, wdt_ref[...])


def ssm_in_proj(h, g_mix, w_in_all, layer, w_dt, *, tile=512):
    m, d = h.shape
    tm = _row_tile(m, tile)
    row = lambda w: pl.BlockSpec((tm, w), lambda i: (i, 0))
    return pl.pallas_call(
        _ssm_in_kernel,
        grid=(m // tm,),
        in_specs=[row(d), _const_spec((1, d)), _layer_spec(w_in_all, layer), _const_spec((d, LANE))],
        out_specs=[row(SSM_INNER), row(SSM_CONV_DIM), row(LANE)],
        out_shape=[jax.ShapeDtypeStruct((m, SSM_INNER), F32),
                   jax.ShapeDtypeStruct((m, SSM_CONV_DIM), F32),
                   jax.ShapeDtypeStruct((m, LANE), F32)],
        compiler_params=_cparams("parallel"),
        name="ssm_in_proj",
    )(h, g_mix.reshape(1, d), w_in_all, w_dt)


def _split3(x):
    x1 = x.astype(BF16)
    r = x - x1.astype(F32)
    x2 = r.astype(BF16)
    x3 = (r - x2.astype(F32)).astype(BF16)
    return x1, x2, x3


def _spread_dot(x, sel):
    x1 = x.astype(BF16)
    x2 = (x - x1.astype(F32)).astype(BF16)
    return _dot(x1, sel) + _dot(x2, sel)


def _exact_dot_left(sel, x):
    x1, x2, x3 = _split3(x)
    return _dot(sel, x1) + _dot(sel, x2) + _dot(sel, x3)


N_SSD_IN = 12


def _ssd_body(z_ref, xbc_ref, dt_ref, cbuf_ref, h0_ref, cw_ref, cb_ref, dtb_ref, alog_ref,
              dskip_ref, ng_ref, spread_ref, y_ref, hf_ref, state_ref, tail_ref, *, valid,
              state_out_every_step=False):
    c = pl.program_id(1)
    q = xbc_ref.shape[0]

    @pl.when(c == 0)
    def _():
        state_ref[...] = h0_ref[...]
        tail_ref[...] = cbuf_ref[...]

    cur = xbc_ref[...]
    prev = tail_ref[...]
    sub = lax.broadcasted_iota(jnp.int32, prev.shape, 0)
    conv = cb_ref[...] + cur * cw_ref[SSM_CONV - 1:SSM_CONV, :]
    for tap in range(SSM_CONV - 1):
        shift = SSM_CONV - 1 - tap
        rolled = pltpu.roll(cur, shift, 0)
        head = jnp.where(sub < shift, pltpu.roll(prev, shift, 0), rolled[0:8])
        shifted = jnp.concatenate([head, rolled[8:]], axis=0)
        conv = conv + shifted * cw_ref[tap:tap + 1, :]
    tail_ref[...] = cur[q - 8:q]
    conv = conv * jax.nn.sigmoid(conv)
    xs = conv[:, :SSM_INNER]
    bm = conv[:, SSM_INNER:SSM_INNER + SSM_GN].astype(BF16)
    cm = conv[:, SSM_INNER + SSM_GN:].astype(BF16)

    row = lax.broadcasted_iota(jnp.int32, (q, q), 0)
    col = lax.broadcasted_iota(jnp.int32, (q, q), 1)
    tri = col <= row
    dt = jax.nn.softplus(dt_ref[...] + dtb_ref[...])
    if valid < q:
        dt = jnp.where(lax.broadcasted_iota(jnp.int32, dt.shape, 0) < valid, dt, 0.0)
    a = dt * (-jnp.exp(alog_ref[...]))
    acum = _exact_dot_left(tri.astype(BF16), a)
    acum_t = acum.T
    spread = spread_ref[...]
    dt_x = _spread_dot(dt, spread)
    grow_x = _spread_dot(jnp.exp(acum), spread)
    wst_x = _spread_dot(jnp.exp(acum[q - 1:q, :] - acum), spread)
    chunk_decay = jnp.exp(acum_t[:, q - 1:q])

    xd = xs * dt_x
    xd_b = xd.astype(BF16)
    xw_b = (xd * wst_x).astype(BF16)
    y_parts = []
    for g in range(SSM_GROUPS):
        ncols = slice(g * SSM_STATE, (g + 1) * SSM_STATE)
        cb = _dot_nt(cm[:, ncols], bm[:, ncols])
        heads = range(g * SSM_HPG, (g + 1) * SSM_HPG)
        s_g = jnp.concatenate([state_ref[h] for h in heads], axis=0)
        gcols = slice(g * SSM_HPG * SSM_HEAD_DIM, (g + 1) * SSM_HPG * SSM_HEAD_DIM)
        y_off = _dot_nt(cm[:, ncols], s_g.astype(BF16)) * grow_x[:, gcols]
        y_diag = []
        for h in heads:
            seg = acum[:, h:h + 1] - acum_t[h:h + 1, :]
            lmat = (cb * jnp.exp(jnp.where(tri, seg, -jnp.inf))).astype(BF16)
            y_diag.append(_dot(lmat, xd_b[:, h * SSM_HEAD_DIM:(h + 1) * SSM_HEAD_DIM]))
        y_parts.append(jnp.concatenate(y_diag, axis=1) + y_off)
        new_states = _dot_tn(xw_b[:, gcols], bm[:, ncols])
        for i, h in enumerate(heads):
            state_ref[h] = state_ref[h] * chunk_decay[h:h + 1, :] + new_states[i * SSM_HEAD_DIM:(i + 1) * SSM_HEAD_DIM, :]
    y = jnp.concatenate(y_parts, axis=1) + xs * dskip_ref[...]
    zf = z_ref[...]
    y = y * (zf * jax.nn.sigmoid(zf))
    gw = SSM_INNER // SSM_GROUPS
    out = []
    for g in range(SSM_GROUPS):
        yg = y[:, g * gw:(g + 1) * gw]
        out.append(_rms(yg, ng_ref[:, g * gw:(g + 1) * gw]))
    y_ref[...] = jnp.concatenate(out, axis=1).astype(BF16)

    if state_out_every_step:
        hf_ref[...] = state_ref[...]
    else:
        @pl.when(c == pl.num_programs(1) - 1)
        def _():
            hf_ref[...] = state_ref[...]


def _ssd_kernel(*refs, valid):
    _ssd_body(*refs, valid=valid)


def _ssd_decode_kernel(pt_ref, *refs, valid, pages, n_q, lam_init, steps_per_seq):
    ssd_in, rest = refs[:N_SSD_IN], refs[N_SSD_IN:]
    lam_ref, g_ref, q_ref, kn_ref, vn_ref = rest[:5]
    k_refs, v_refs = rest[5:5 + pages], rest[5 + pages:5 + 2 * pages]
    y_ref, hf_ref, o_ref, state_ref, tail_ref, m_ref, l_ref, acc_ref = rest[5 + 2 * pages:]
    t = pl.program_id(0) * pl.num_programs(1) + pl.program_id(1)
    pl.when(t == 0)(functools.partial(_zero_decode_stats, m_ref, l_ref, acc_ref))
    _ssd_body(*ssd_in, y_ref, hf_ref, state_ref, tail_ref, valid=valid, state_out_every_step=True)
    part = t % steps_per_seq
    _decode_step(lam_ref, g_ref, q_ref, kn_ref, vn_ref, k_refs, v_refs, o_ref, m_ref, l_ref, acc_ref,
                 first=part == 0, last=part == steps_per_seq - 1, n_q=n_q, lam_init=lam_init)


def ssd_scan(z, xbc, dt_raw, conv_buf, h0, conv_w, conv_b, dt_bias, a_log, d_skip, norm_g, *,
             batch, chunk_len, valid, decode=None):
    m = z.shape[0]
    seq = m // batch
    q = chunk_len
    assert seq % q == 0 and (valid == q or seq == q)
    nc = seq // q
    z3 = z.reshape(batch, seq, SSM_INNER)
    x3 = xbc.reshape(batch, seq, SSM_CONV_DIM)
    d3 = dt_raw.reshape(batch, seq, LANE)
    cbuf = jnp.pad(conv_buf, ((0, 0), (8 - (SSM_CONV - 1), 0), (0, 0)))
    spread = (jnp.arange(SSM_INNER)[None, :] // SSM_HEAD_DIM == jnp.arange(LANE)[:, None]).astype(BF16)
    lane_pad = lambda v: jnp.pad(v.reshape(1, -1), ((0, 0), (0, LANE - v.shape[-1])))
    chunk = lambda w: pl.BlockSpec((None, q, w), lambda b, c, *_: (b, c, 0))
    state_spec = pl.BlockSpec((None, SSM_HEADS, SSM_HEAD_DIM, SSM_STATE), lambda b, c, *_: (b, 0, 0, 0))
    const = lambda shape: pl.BlockSpec(shape, lambda b, c, *_: (0,) * len(shape))
    in_specs = [chunk(SSM_INNER), chunk(SSM_CONV_DIM), chunk(LANE),
                pl.BlockSpec((None, 8, SSM_CONV_DIM), lambda b, c, *_: (b, 0, 0)), state_spec,
                const((SSM_CONV, SSM_CONV_DIM)), const((1, SSM_CONV_DIM)), const((1, LANE)),
                const((1, LANE)), const((1, SSM_INNER)), const((1, SSM_INNER)),
                const((LANE, SSM_INNER))]
    assert len(in_specs) == N_SSD_IN
    operands = (z3, x3, d3, cbuf, h0, conv_w, conv_b.reshape(1, -1), lane_pad(dt_bias), lane_pad(a_log),
                jnp.repeat(d_skip, SSM_HEAD_DIM).reshape(1, -1), norm_g.reshape(1, -1), spread)
    out_specs = [chunk(SSM_INNER), state_spec]
    out_shape = [jax.ShapeDtypeStruct((batch, seq, SSM_INNER), BF16), jax.ShapeDtypeStruct(h0.shape, F32)]
    scratch = [pltpu.VMEM((SSM_HEADS, SSM_HEAD_DIM, SSM_STATE), F32), pltpu.VMEM((8, SSM_CONV_DIM), F32)]
    if decode is None:
        y, hf = pl.pallas_call(
            functools.partial(_ssd_kernel, valid=valid),
            grid=(batch, nc), in_specs=in_specs, out_specs=out_specs, out_shape=out_shape,
            scratch_shapes=scratch, compiler_params=_cparams("parallel", "arbitrary"), name="ssd_scan",
        )(*operands)
        return y.reshape(m, SSM_INNER), hf

    b_dec, n_q, hw = decode["qb"].shape
    n_pages = decode["page_table"].shape[1]
    steps_per_seq, pages = _decode_split(batch * nc, b_dec, n_pages)
    pt, qrows, kn, vn, ck, cv = _decode_operands(decode["qb"], decode["kb_new"], decode["vb_new"],
                                                 decode["cache_k"], decode["cache_v"],
                                                 decode["page_table"], decode["layer"])
    n_rows = qrows.shape[1]
    step_of = lambda b, c: b * nc + c

    def page_spec(i):
        def index(b, c, pt_ref):
            t = step_of(b, c)
            return (pt_ref[(t // steps_per_seq) * n_pages + (t % steps_per_seq) * pages + i], 0, 0)
        return pl.BlockSpec((None,) + ck.shape[1:], index)

    per_seq = lambda r, w: pl.BlockSpec((None, r, w), lambda b, c, pt_ref: (step_of(b, c) // steps_per_seq, 0, 0))
    dec_specs = [const((4, ATTN_DK)), const((1, ATTN_DV)), per_seq(n_rows, ATTN_HD), per_seq(NEW_ROWS, hw),
                 per_seq(NEW_ROWS, hw)] + [page_spec(i) for i in range(pages)] * 2
    grid_spec = pltpu.PrefetchScalarGridSpec(
        num_scalar_prefetch=1, grid=(batch, nc),
        in_specs=in_specs + dec_specs,
        out_specs=out_specs + [per_seq(n_q, hw)],
        scratch_shapes=scratch + _decode_scratch(n_q),
    )
    y, hf, o_dec = pl.pallas_call(
        functools.partial(_ssd_decode_kernel, valid=valid, pages=pages, n_q=n_q,
                          lam_init=decode["lam_init"], steps_per_seq=steps_per_seq),
        grid_spec=grid_spec,
        out_shape=out_shape + [jax.ShapeDtypeStruct((b_dec, n_q, hw), BF16)],
        compiler_params=_cparams("arbitrary", "arbitrary"),
        name="ssd_scan_decode",
    )(pt, *operands, decode["lam_p"], decode["subln_g"].reshape(1, ATTN_DV), qrows, kn, vn,
      *([ck] * pages), *([cv] * pages))
    return y.reshape(m, SSM_INNER), hf, o_dec


def _ssm_group(h, batch, seq, conv_buf, h0, g_mix, w_in_all, layer, w_dt, conv_w, conv_b, dt_bias, a_log,
               d_skip, norm_g, decode=None):
    d = h.shape[1]
    if seq % SSM_CHUNK == 0:
        padded = seq
        chunk_len = valid = SSM_CHUNK
        hin = h
    else:
        assert seq <= SSM_SHORT_CHUNK
        padded = chunk_len = SSM_SHORT_CHUNK
        valid = seq
        hin = jnp.pad(h.reshape(batch, seq, d), ((0, 0), (0, padded - seq), (0, 0))).reshape(-1, d)
    z, xbc, dt = ssm_in_proj(hin, g_mix, w_in_all, layer, w_dt)
    y, hf, *dec_out = ssd_scan(z, xbc, dt, conv_buf, h0, conv_w, conv_b, dt_bias, a_log, d_skip, norm_g,
                               batch=batch, chunk_len=chunk_len, valid=valid, decode=decode)
    xbc3 = xbc.reshape(batch, padded, SSM_CONV_DIM)[:, :seq]
    new_buf = jnp.concatenate([conv_buf, xbc3], axis=1)[:, seq:]
    y = y.reshape(batch, padded, SSM_INNER)[:, :seq].reshape(batch * seq, SSM_INNER)
    return (y, new_buf, hf, *dec_out)


def kernel(x_prompt, x_sample, cache_k, cache_v, state_ssm, state_conv, page_table, p_prompt, p_sample, norm_mix_g, norm_mlp_g, norm_ple_g, mlp_w_up, mlp_w_down, ple_w_gate, ple_w_proj, final_norm_g, sgu_w_in, sgu_ln_g, sgu_ln_b, sgu_w_s, sgu_b_s, sgu_w_out, attn_w_qkv, attn_lambda, attn_subln_g, attn_w_out, ssm_w_in, ssm_conv_w, ssm_conv_b, ssm_dt_bias, ssm_a_log, ssm_d, ssm_norm_g, ssm_w_out):
    b_p, l_p, d = x_prompt.shape
    b_s, l_s, _ = x_sample.shape
    depth = norm_mix_g.shape[0]
    past_len = page_table.shape[1] * cache_k.shape[2]
    m_s = b_s * l_s
    assert m_s % SGU_CHUNK == 0 and SGU_CHUNK % l_s == 0 and l_p % SGU_CHUNK == 0
    bf = lambda w: w.astype(BF16)

    hp = x_prompt.reshape(b_p * l_p, d)
    hs = x_sample.reshape(m_s, d)
    rope_p = _rope_tables(jnp.arange(l_p, dtype=jnp.int32))
    rope_s = _rope_tables(past_len + jnp.arange(l_s, dtype=jnp.int32))

    w_up, w_down, w_gate, w_proj = bf(mlp_w_up), bf(mlp_w_down), bf(ple_w_gate), bf(ple_w_proj)
    w_sgu_in, w_sgu_out = bf(sgu_w_in), bf(sgu_w_out)
    w_qkv, w_attn_out = bf(attn_w_qkv), bf(attn_w_out)
    w_ssm_in, w_ssm_out = bf(ssm_w_in), bf(ssm_w_out)
    pp = p_prompt.reshape(depth, b_p * l_p, D_PLE)
    ps = p_sample.reshape(depth, m_s, D_PLE)

    k_p, v_p, k_s, v_s = [], [], [], []
    ssm_p, conv_p, ssm_s, conv_s = [], [], [], []
    sgu_s = []

    def channel(h, mix, p_all, i, w_mix, j, tile):
        return channel_update(h, mix, p_all, i, w_mix, j, norm_mlp_g[i], w_up, w_down, norm_ple_g[i],
                              w_gate, w_proj, final_norm_g, final_norm=i == depth - 1, tile=tile)

    def decode_rides_along(i):
        if i + 1 >= depth or (i + 1) % N_MIXERS != 2 or l_p % SSM_CHUNK:
            return False
        return _decode_split(b_p * (l_p // SSM_CHUNK), b_s, page_table.shape[1]) is not None

    pending = None
    for i in range(depth):
        kind, j = i % N_MIXERS, i // N_MIXERS
        g_mix = norm_mix_g[i]
        if kind == 0:
            ws_p, bs_p = _sgu_spatial(sgu_w_s[j], sgu_b_s[j], l_p)
            ws_s, bs_s = _sgu_spatial(sgu_w_s[j], sgu_b_s[j], l_s)
            mix_p, _ = sgu_mix(hp, g_mix, w_sgu_in, j, sgu_ln_g[j], sgu_ln_b[j], ws_p, bs_p, want_v=False)
            mix_s, v_rows = sgu_mix(hs, g_mix, w_sgu_in, j, sgu_ln_g[j], sgu_ln_b[j], ws_s, bs_s,
                                    want_v=True, tile=SGU_CHUNK)
            sgu_s.append(v_rows.reshape(b_s, l_s, SGU_WIDTH))
            w_mix = w_sgu_out
        elif kind == 1:
            lam_init = 0.8 - 0.6 * math.exp(-0.3 * i)
            qb, k, v, kb, vb = qkv_rope(hp, g_mix, w_qkv, j, *rope_p, transposed_v=True)
            mix_p = attn_prompt(qb, kb, vb, attn_lambda[j], attn_subln_g[j], batch=b_p, lam_init=lam_init)
            k_p.append(k.reshape(b_p, l_p, ATTN_HEADS, ATTN_HD))
            v_p.append(v.reshape(b_p, l_p, ATTN_HEADS, ATTN_DV))
            qb, k, v, kb, vb = qkv_rope(hs, g_mix, w_qkv, j, *rope_s, tile=SGU_CHUNK)
            k_s.append(k.reshape(b_s, l_s, ATTN_HEADS, ATTN_HD))
            v_s.append(v.reshape(b_s, l_s, ATTN_HEADS, ATTN_DV))
            seq3 = lambda x: x.reshape(b_s, l_s, -1)
            dec = dict(qb=seq3(qb), kb_new=seq3(kb), vb_new=seq3(vb), cache_k=cache_k, cache_v=cache_v,
                       page_table=page_table, layer=j, lam_p=attn_lambda[j], subln_g=attn_subln_g[j],
                       lam_init=lam_init)
            w_mix = w_attn_out
            if decode_rides_along(i):
                pending = (dec, (ps, i, w_mix, j))
                mix_s = None
            else:
                mix_s = attn_decode(**dec).reshape(m_s, -1)
        else:
            w_dt = jnp.pad(w_ssm_in[j][:, SSM_INNER + SSM_CONV_DIM:], ((0, 0), (0, LANE - SSM_HEADS)))
            shared = (g_mix, w_ssm_in, j, w_dt, ssm_conv_w[j], ssm_conv_b[j], ssm_dt_bias[j],
                      ssm_a_log[j], ssm_d[j], ssm_norm_g[j])
            zero_buf = jnp.zeros((b_p, SSM_CONV - 1, SSM_CONV_DIM), F32)
            zero_h = jnp.zeros((b_p, SSM_HEADS, SSM_HEAD_DIM, SSM_STATE), F32)
            if pending is None:
                mix_p, cb_p, hf_p = _ssm_group(hp, b_p, l_p, zero_buf, zero_h, *shared)
            else:
                dec, prev_channel = pending
                pending = None
                mix_p, cb_p, hf_p, o_dec = _ssm_group(hp, b_p, l_p, zero_buf, zero_h, *shared, decode=dec)
                hs = channel(hs, o_dec.reshape(m_s, -1), *prev_channel, SGU_CHUNK)
            mix_s, cb_s, hf_s = _ssm_group(hs, b_s, l_s, state_conv[j], state_ssm[j], *shared)
            ssm_p.append(hf_p)
            conv_p.append(cb_p)
            ssm_s.append(hf_s)
            conv_s.append(cb_s)
            w_mix = w_ssm_out
        hp = channel(hp, mix_p, pp, i, w_mix, j, 512)
        if mix_s is not None:
            hs = channel(hs, mix_s, ps, i, w_mix, j, SGU_CHUNK)

    return (hp.reshape(b_p, l_p, d), hs.reshape(b_s, l_s, d),
            jnp.stack(k_p), jnp.stack(v_p), jnp.stack(k_s), jnp.stack(v_s),
            jnp.stack(ssm_p), jnp.stack(conv_p), jnp.stack(ssm_s), jnp.stack(conv_s),
            jnp.stack(sgu_s))
```

```python
import functools
import math

import jax
import jax.numpy as jnp
from jax import lax
from jax.experimental import pallas as pl
from jax.experimental.pallas import tpu as pltpu

F32 = jnp.float32
BF16 = jnp.bfloat16

EPS = 1e-6
D_MODEL = 1024
D_FF = 4 * D_MODEL
D_PLE = 256
N_MIXERS = 3

SGU_CHUNK = 128
SGU_WIDTH = 2 * D_MODEL
SGU_GROUPS = 8
SGU_GDIM = SGU_WIDTH // SGU_GROUPS

ATTN_HEADS = 8
ATTN_DK = 64
ATTN_DV = 128
ATTN_HD = 2 * ATTN_DK
ROPE_THETA = 10000.0

SSM_INNER = 2 * D_MODEL
SSM_HEAD_DIM = 64
SSM_HEADS = SSM_INNER // SSM_HEAD_DIM
SSM_GROUPS = 4
SSM_STATE = 128
SSM_CONV = 4
SSM_CHUNK = 128
SSM_SHORT_CHUNK = 16
SSM_GN = SSM_GROUPS * SSM_STATE
SSM_CONV_DIM = SSM_INNER + 2 * SSM_GN
SSM_HPG = SSM_HEADS // SSM_GROUPS

VMEM_LIMIT_BYTES = 56 * 1024 * 1024
LANE = 128


def _cparams(*sem):
    return pltpu.CompilerParams(dimension_semantics=sem, vmem_limit_bytes=VMEM_LIMIT_BYTES)


def _const_spec(shape):
    zeros = (0,) * len(shape)
    return pl.BlockSpec(shape, lambda *_: zeros, pipeline_mode=pl.Buffered(1))


def _layer_spec(stack, layer):
    tail = tuple(stack.shape[1:])
    zeros = (0,) * len(tail)
    return pl.BlockSpec((None,) + tail, lambda *_: (layer,) + zeros, pipeline_mode=pl.Buffered(1))


def _rms(x, g):
    ms = jnp.mean(x * x, axis=-1, keepdims=True)
    return (x * lax.rsqrt(ms + EPS)) * g


def _dot(a, b):
    return jnp.dot(a, b, preferred_element_type=F32)


def _dot_nt(a, b):
    return lax.dot_general(a, b, (((1,), (1,)), ((), ())), preferred_element_type=F32)


def _dot_tn(a, b):
    return lax.dot_general(a, b, (((0,), (0,)), ((), ())), preferred_element_type=F32)


def _row_tile(m, want):
    t = min(m, want)
    assert m % t == 0, (m, t)
    return t


def _channel_kernel(h_ref, mix_ref, p_ref, wmix_ref, gmlp_ref, wup_ref, wdown_ref, gple_ref,
                    wgate_ref, wproj_ref, gfin_ref, out_ref, *, final_norm):
    h = h_ref[...] + _dot(mix_ref[...], wmix_ref[...])
    xn = _rms(h, gmlp_ref[...]).astype(BF16)
    a = jnp.square(jnp.maximum(_dot(xn, wup_ref[...]), 0.0)).astype(BF16)
    h = h + _dot(a, wdown_ref[...])
    xn = _rms(h, gple_ref[...]).astype(BF16)
    gate = jax.nn.sigmoid(_dot(xn, wgate_ref[...]))
    h = h + gate * _dot(p_ref[...].astype(BF16), wproj_ref[...])
    if final_norm:
        h = _rms(h, gfin_ref[...])
    out_ref[...] = h


def channel_update(h, mix, p_all, layer, w_mix_all, mix_layer, g_mlp, w_up_all, w_down_all, g_ple,
                   w_gate_all, w_proj_all, g_fin, *, final_norm, tile=512):
    m, d = h.shape
    dm = mix.shape[1]
    tm = _row_tile(m, tile)
    row = lambda w: pl.BlockSpec((tm, w), lambda i: (i, 0))
    return pl.pallas_call(
        functools.partial(_channel_kernel, final_norm=final_norm),
        grid=(m // tm,),
        in_specs=[row(d), row(dm), pl.BlockSpec((None, tm, D_PLE), lambda i: (layer, i, 0)),
                  _layer_spec(w_mix_all, mix_layer), _const_spec((1, d)),
                  _layer_spec(w_up_all, layer), _layer_spec(w_down_all, layer), _const_spec((1, d)),
                  _layer_spec(w_gate_all, layer), _layer_spec(w_proj_all, layer), _const_spec((1, d))],
        out_specs=row(d),
        out_shape=jax.ShapeDtypeStruct((m, d), F32),
        compiler_params=_cparams("parallel"),
        name="channel_update",
    )(h, mix, p_all, w_mix_all, g_mlp.reshape(1, d), w_up_all, w_down_all, g_ple.reshape(1, d),
      w_gate_all, w_proj_all, g_fin.reshape(1, d))


def _sgu_kernel(h_ref, g_ref, win_ref, lng_ref, lnb_ref, ws_ref, bs_ref, mix_ref, *v_out,
                n_chunks):
    xn = _rms(h_ref[...], g_ref[...]).astype(BF16)
    v = jax.nn.gelu(_dot(xn, win_ref[:, SGU_WIDTH:]))
    u = jax.nn.gelu(_dot(xn, win_ref[:, :SGU_WIDTH]))
    mu = jnp.mean(v, axis=-1, keepdims=True)
    vc = v - mu
    v = (vc * lax.rsqrt(jnp.mean(vc * vc, axis=-1, keepdims=True) + EPS)) * lng_ref[...] + lnb_ref[...]
    if v_out:
        v_out[0][...] = v
    vb = v.astype(BF16)
    bs = bs_ref[...]
    for c in range(n_chunks):
        rows = slice(c * SGU_CHUNK, (c + 1) * SGU_CHUNK)
        for g in range(SGU_GROUPS):
            cols = slice(g * SGU_GDIM, (g + 1) * SGU_GDIM)
            s = _dot(ws_ref[g], vb[rows, cols]) + bs[:, g:g + 1]
            mix_ref[rows, cols] = (u[rows, cols] * s).astype(BF16)


def _sgu_spatial(w_s, b_s, seq):
    q = min(SGU_CHUNK, seq)
    tri = jnp.tril(jnp.ones((q, q), dtype=bool))
    ws = jnp.where(tri, w_s[:, :q, :q], 0)
    bs = b_s[:, :q]
    rep = SGU_CHUNK // q
    if rep > 1:
        eye = jnp.eye(rep, dtype=ws.dtype)
        ws = jnp.einsum("ab,gts->gatbs", eye, ws).reshape(SGU_GROUPS, SGU_CHUNK, SGU_CHUNK)
        bs = jnp.tile(bs, (1, rep))
    return ws.astype(BF16), bs.T


def sgu_mix(h, g_mix, w_in_all, layer, ln_g, ln_b, ws, bs, *, want_v, tile=256):
    m, d = h.shape
    tm = _row_tile(m, tile)
    row = lambda w: pl.BlockSpec((tm, w), lambda i: (i, 0))
    out_shape = [jax.ShapeDtypeStruct((m, SGU_WIDTH), BF16)]
    out_specs = [row(SGU_WIDTH)]
    if want_v:
        out_shape.append(jax.ShapeDtypeStruct((m, SGU_WIDTH), F32))
        out_specs.append(row(SGU_WIDTH))
    outs = pl.pallas_call(
        functools.partial(_sgu_kernel, n_chunks=tm // SGU_CHUNK),
        grid=(m // tm,),
        in_specs=[row(d), _const_spec((1, d)), _layer_spec(w_in_all, layer),
                  _const_spec((1, SGU_WIDTH)), _const_spec((1, SGU_WIDTH)),
                  _const_spec((SGU_GROUPS, SGU_CHUNK, SGU_CHUNK)),
                  _const_spec((SGU_CHUNK, SGU_GROUPS))],
        out_specs=out_specs,
        out_shape=out_shape,
        compiler_params=_cparams("parallel"),
        name="sgu_mix",
    )(h, g_mix.reshape(1, d), w_in_all, ln_g.reshape(1, -1), ln_b.reshape(1, -1), ws, bs)
    return outs if want_v else (outs[0], None)


def _rope_tables(pos):
    half = ATTN_DK // 2
    inv = ROPE_THETA ** (-jnp.arange(half, dtype=F32) / half)
    ang = pos.astype(F32)[:, None] * inv[None, :]
    cos, sin = jnp.cos(ang), jnp.sin(ang)
    cos_t = jnp.tile(cos, (1, 4))
    sin_t = jnp.tile(jnp.concatenate([-sin, sin], axis=1), (1, 2))
    return cos_t, sin_t


def _rope_head(x, cos, sin_signed, first_half):
    partner = jnp.where(first_half, pltpu.roll(x, LANE - ATTN_DK // 2, 1),
                        pltpu.roll(x, ATTN_DK // 2, 1))
    return x * cos + partner * sin_signed


def _qkv_kernel(h_ref, g_ref, w_ref, cos_ref, sin_ref, qb_ref, k_ref, v_ref, kb_ref, vb_ref, *,
                transposed_v):
    xn = _rms(h_ref[...], g_ref[...]).astype(BF16)
    qkv = _dot(xn, w_ref[...])
    cos = cos_ref[...]
    sin = sin_ref[...]
    lane = lax.broadcasted_iota(jnp.int32, cos.shape, 1)
    first_half = (lane % ATTN_DK) < (ATTN_DK // 2)
    qk_dim = ATTN_HEADS * ATTN_HD
    scale = ATTN_DK ** -0.5 * math.log2(math.e)
    rows = h_ref.shape[0]
    for hd in range(ATTN_HEADS):
        cols = slice(hd * ATTN_HD, (hd + 1) * ATTN_HD)
        q = _rope_head(qkv[:, cols], cos, sin, first_half)
        k = _rope_head(qkv[:, qk_dim + hd * ATTN_HD:qk_dim + (hd + 1) * ATTN_HD], cos, sin,
                       first_half)
        qb_ref[:, cols] = (q * scale).astype(BF16)
        k_ref[pl.ds(hd, rows, stride=ATTN_HEADS), :] = k
        kb_ref[:, cols] = k.astype(BF16)
        v_ref[pl.ds(hd, rows, stride=ATTN_HEADS), :] = qkv[:, 2 * qk_dim + hd * ATTN_DV:
                                                           2 * qk_dim + (hd + 1) * ATTN_DV]
    v = qkv[:, 2 * qk_dim:]
    vb_ref[...] = (v.T if transposed_v else v).astype(BF16)


def qkv_rope(h, g_mix, w_qkv_all, layer, cos_t, sin_t, *, transposed_v=False, tile=512):
    m, d = h.shape
    seq = cos_t.shape[0]
    tm = _row_tile(m, tile)
    row = lambda w: pl.BlockSpec((tm, w), lambda i: (i, 0))
    if seq % tm == 0:
        n_blk = seq // tm
        table = pl.BlockSpec((tm, LANE), lambda i: (i % n_blk, 0))
    else:
        assert tm % seq == 0, (tm, seq)
        cos_t = jnp.tile(cos_t, (tm // seq, 1))
        sin_t = jnp.tile(sin_t, (tm // seq, 1))
        table = _const_spec((tm, LANE))
    hw = ATTN_HEADS * ATTN_HD
    vb_spec, vb_shape = row(hw), jax.ShapeDtypeStruct((m, hw), BF16)
    head_rows = pl.BlockSpec((tm * ATTN_HEADS, ATTN_HD), lambda i: (i, 0))
    if transposed_v:
        assert seq % tm == 0
        n_blk = seq // tm
        vb_spec = pl.BlockSpec((None, hw, tm), lambda i: (i // n_blk, 0, i % n_blk))
        vb_shape = jax.ShapeDtypeStruct((m // seq, hw, seq), BF16)
    return pl.pallas_call(
        functools.partial(_qkv_kernel, transposed_v=transposed_v),
        grid=(m // tm,),
        in_specs=[row(d), _const_spec((1, d)), _layer_spec(w_qkv_all, layer), table, table],
        out_specs=[row(hw), head_rows, head_rows, row(hw), vb_spec],
        out_shape=[jax.ShapeDtypeStruct((m, hw), BF16),
                   jax.ShapeDtypeStruct((m * ATTN_HEADS, ATTN_HD), F32),
                   jax.ShapeDtypeStruct((m * ATTN_HEADS, ATTN_DV), F32),
                   jax.ShapeDtypeStruct((m, hw), BF16), vb_shape],
        compiler_params=_cparams("parallel"),
        name="qkv_rope",
    )(h, g_mix.reshape(1, d), w_qkv_all, cos_t, sin_t)


def _lambda_value(lam_ref, lam_init):
    lp = lam_ref[...]
    s1 = jnp.sum(lp[0:1] * lp[1:2], axis=-1, keepdims=True)
    s2 = jnp.sum(lp[2:3] * lp[3:4], axis=-1, keepdims=True)
    return jnp.exp(s1) - jnp.exp(s2) + lam_init


def _softmax_step_t(s, v, m_prev, acc_prev):
    m_new = jnp.maximum(m_prev, jnp.max(s, axis=0, keepdims=True))
    alpha = jnp.exp2(m_prev - m_new)
    p = jnp.exp2(s - m_new)
    acc = alpha * acc_prev + _dot(v, p.astype(BF16))
    return m_new, acc


def _attn_prompt_kernel(lam_ref, g_ref, q_ref, k_ref, v_ref, o_ref, *, tile, n_tiles, lam_init):
    qi = pl.program_id(2)
    half = tile // 2
    q = q_ref[...]
    lane = lax.broadcasted_iota(jnp.int32, q.shape, 1)
    zero = jnp.zeros_like(q)
    q1 = jnp.where(lane < ATTN_DK, q, zero)
    q2 = jnp.where(lane < ATTN_DK, zero, q)

    def block(start, n_keys, q_lo, carry, masked):
        k = k_ref[pl.ds(start, n_keys), :]
        ones_row = (lax.broadcasted_iota(jnp.int32, (ACC_PAD, n_keys), 0) == 0).astype(BF16)
        v = jnp.concatenate([v_ref[:, pl.ds(start, n_keys)], ones_row], axis=0)
        s1 = _dot_nt(k, q1[q_lo:])
        s2 = _dot_nt(k, q2[q_lo:])
        if masked:
            kr = lax.broadcasted_iota(jnp.int32, s1.shape, 0)
            qc = lax.broadcasted_iota(jnp.int32, s1.shape, 1)
            s1 = jnp.where(kr <= qc, s1, -jnp.inf)
            s2 = jnp.where(kr <= qc, s2, -jnp.inf)
        c1, c2 = carry
        return _softmax_step_t(s1, v, *c1), _softmax_step_t(s2, v, *c2)

    init = (jnp.full((1, tile), -jnp.inf, F32), jnp.zeros((ATTN_DV + ACC_PAD, tile), F32))
    lam = _lambda_value(lam_ref, lam_init)

    def tile_at(n_full):
        carry = (init, init)
        for ki in range(n_full):
            carry = block(ki * tile, tile, 0, carry, False)
        base = n_full * tile
        carry = block(base, half, 0, carry, True)
        tail = tuple(tuple(x[:, half:] for x in c) for c in carry)
        tail = block(base + half, half, half, tail, True)
        (_, a1), (_, a2) = tuple(
            tuple(jnp.concatenate([x[:, :half], y], axis=1) for x, y in zip(c, t)) for c, t in zip(carry, tail))
        normed = lambda a: a[:ATTN_DV] / a[ATTN_DV:ATTN_DV + 1]
        o = (normed(a1) - lam * normed(a2)).T
        o = _rms(o, g_ref[...]) * (1.0 - lam_init)
        o_ref[...] = o.astype(BF16)

    for n_full in range(n_tiles):
        pl.when(qi == n_full)(functools.partial(tile_at, n_full))


def attn_prompt(qb, kb, vb_t, lam_p, subln_g, *, batch, lam_init, tile=1024):
    m, hw = qb.shape
    seq = m // batch
    t = _row_tile(seq, tile)
    q3, k3 = (x.reshape(batch, seq, hw) for x in (qb, kb))
    q_spec = pl.BlockSpec((None, t, ATTN_HD), lambda b, h, qi: (b, qi, h))
    k_spec = pl.BlockSpec((None, seq, ATTN_HD), lambda b, h, qi: (b, 0, h))
    vt_spec = pl.BlockSpec((None, ATTN_DV, seq), lambda b, h, qi: (b, h, 0))
    const = lambda shape: pl.BlockSpec(shape, lambda b, h, qi: (0,) * len(shape))
    out = pl.pallas_call(
        functools.partial(_attn_prompt_kernel, tile=t, n_tiles=seq // t, lam_init=lam_init),
        grid=(batch, ATTN_HEADS, seq // t),
        in_specs=[const((4, ATTN_DK)), const((1, ATTN_DV)), q_spec, k_spec, vt_spec],
        out_specs=q_spec,
        out_shape=jax.ShapeDtypeStruct((batch, seq, hw), BF16),
        compiler_params=_cparams("parallel", "parallel", "parallel"),
        name="attn_prompt",
    )(lam_p, subln_g.reshape(1, ATTN_DV), q3, k3, vb_t)
    return out.reshape(m, hw)


NEW_ROWS = 16
ACC_PAD = 16


def _decode_step(lam_ref, g_ref, q_ref, kn_ref, vn_ref, k_refs, v_refs, o_ref, m_ref, l_ref, acc_ref, *,
                 first, last, n_q, lam_init):
    rows = 2 * n_q
    q_all = q_ref[...].astype(BF16)

    def update(s, pv_fn, fresh):
        m_prev = jnp.where(fresh, -jnp.inf, m_ref[...])
        m_new = jnp.maximum(m_prev, jnp.max(s, axis=1, keepdims=True))
        alpha = jnp.exp2(m_prev - m_new)
        p = jnp.exp2(s - m_new)
        l_ref[...] = jnp.where(fresh, 0.0, alpha * l_ref[...]) + jnp.sum(p, axis=1, keepdims=True)
        acc_ref[...] = jnp.where(fresh, 0.0, alpha * acc_ref[...]) + pv_fn(p)
        m_ref[...] = m_new

    n_cols = k_refs[0].shape[0]
    r = lax.broadcasted_iota(jnp.int32, (q_all.shape[0], n_cols), 0)
    c = lax.broadcasted_iota(jnp.int32, (q_all.shape[0], n_cols), 1)
    own_head = (c % ATTN_HEADS) == (r // rows)
    s = jnp.concatenate(
        [jnp.where(own_head, _dot_nt(q_all, k_ref[...].astype(BF16)), -jnp.inf) for k_ref in k_refs],
        axis=1)

    def pv_pages(p):
        acc = None
        for i, v_ref in enumerate(v_refs):
            d = _dot(p[:, i * n_cols:(i + 1) * n_cols].astype(BF16), v_ref[...].astype(BF16))
            acc = d if acc is None else acc + d
        return acc

    update(s, pv_pages, first)

    @pl.when(last)
    def _():
        kn = kn_ref[...]
        vn = vn_ref[...]
        head_cols = lambda x, h: x[:, h * ATTN_HD:(h + 1) * ATTN_HD]
        q_heads = [q_all[h * rows:(h + 1) * rows] for h in range(ATTN_HEADS)]
        s_new = jnp.concatenate([_dot_nt(q_heads[h], head_cols(kn, h)) for h in range(ATTN_HEADS)], axis=0)
        rn = lax.broadcasted_iota(jnp.int32, s_new.shape, 0)
        cn = lax.broadcasted_iota(jnp.int32, s_new.shape, 1)
        s_new = jnp.where(cn <= rn % n_q, s_new, -jnp.inf)

        def pv_new(p):
            return jnp.concatenate(
                [_dot(p[h * rows:(h + 1) * rows].astype(BF16), head_cols(vn, h)) for h in range(ATTN_HEADS)],
                axis=0)

        update(s_new, pv_new, False)
        lam = _lambda_value(lam_ref, lam_init)
        o_all = acc_ref[...] / l_ref[...]
        for h in range(ATTN_HEADS):
            r1 = h * rows
            o = o_all[r1:r1 + n_q] - lam * o_all[r1 + n_q:r1 + rows]
            o = _rms(o, g_ref[...]) * (1.0 - lam_init)
            o_ref[:, h * ATTN_DV:(h + 1) * ATTN_DV] = o.astype(BF16)


def _zero_decode_stats(m_ref, l_ref, acc_ref):
    m_ref[...] = jnp.zeros(m_ref.shape, F32)
    l_ref[...] = jnp.zeros(l_ref.shape, F32)
    acc_ref[...] = jnp.zeros(acc_ref.shape, F32)


def _attn_decode_kernel(pt_ref, lam_ref, g_ref, q_ref, kn_ref, vn_ref, *refs, pages, n_q, lam_init):
    o_ref, m_ref, l_ref, acc_ref = refs[2 * pages:]
    step = pl.program_id(1)
    pl.when(step == 0)(functools.partial(_zero_decode_stats, m_ref, l_ref, acc_ref))
    _decode_step(lam_ref, g_ref, q_ref, kn_ref, vn_ref, refs[:pages], refs[pages:2 * pages], o_ref,
                 m_ref, l_ref, acc_ref, first=step == 0, last=step == pl.num_programs(1) - 1,
                 n_q=n_q, lam_init=lam_init)


def _decode_operands(qb, kb_new, vb_new, cache_k, cache_v, page_table, layer):
    b, n_q, hw = qb.shape
    n_layers, n_pool, page, n_heads, hd = cache_k.shape
    assert n_q <= NEW_ROWS and (n_heads, hd) == (ATTN_HEADS, ATTN_HD)
    ck = cache_k.reshape(n_layers * n_pool, page * n_heads, hd)
    cv = cache_v.reshape(n_layers * n_pool, page * n_heads, hd)
    pt = (page_table.astype(jnp.int32) + layer * n_pool).reshape(-1)
    half = (jnp.arange(ATTN_HD) // ATTN_DK)[None, :] == jnp.arange(2)[:, None]
    q4 = qb.reshape(b, n_q, ATTN_HEADS, ATTN_HD).astype(F32)
    qrows = jnp.where(half[None, None, :, None, :], jnp.transpose(q4, (0, 2, 1, 3))[:, :, None], 0.0)
    qrows = qrows.reshape(b, ATTN_HEADS * 2 * n_q, ATTN_HD)
    pad = ((0, 0), (0, NEW_ROWS - n_q), (0, 0))
    return pt, qrows, jnp.pad(kb_new, pad), jnp.pad(vb_new, pad), ck, cv


def _decode_scratch(n_q):
    n_rows = ATTN_HEADS * 2 * n_q
    return [pltpu.VMEM((n_rows, 1), F32), pltpu.VMEM((n_rows, 1), F32), pltpu.VMEM((n_rows, ATTN_DV), F32)]


def _decode_split(steps, n_seq, n_pages):
    if steps % n_seq:
        return None
    steps_per_seq = steps // n_seq
    if n_pages % steps_per_seq:
        return None
    return steps_per_seq, n_pages // steps_per_seq


def attn_decode(qb, kb_new, vb_new, cache_k, cache_v, page_table, layer, lam_p, subln_g, *,
                lam_init, pages=8):
    b, n_q, hw = qb.shape
    n_pages = page_table.shape[1]
    assert n_pages % pages == 0
    pt, qrows, kn, vn, ck, cv = _decode_operands(qb, kb_new, vb_new, cache_k, cache_v, page_table, layer)
    n_rows = qrows.shape[1]

    def page_spec(i):
        return pl.BlockSpec((None,) + ck.shape[1:],
                            lambda bi, s, pt_ref: (pt_ref[bi * n_pages + s * pages + i], 0, 0))

    per_b = lambda r, w: pl.BlockSpec((None, r, w), lambda bi, s, pt_ref: (bi, 0, 0))
    const = lambda shape: pl.BlockSpec(shape, lambda bi, s, pt_ref: (0,) * len(shape))
    grid_spec = pltpu.PrefetchScalarGridSpec(
        num_scalar_prefetch=1,
        grid=(b, n_pages // pages),
        in_specs=[const((4, ATTN_DK)), const((1, ATTN_DV)), per_b(n_rows, ATTN_HD), per_b(NEW_ROWS, hw),
                  per_b(NEW_ROWS, hw)] + [page_spec(i) for i in range(pages)] * 2,
        out_specs=per_b(n_q, hw),
        scratch_shapes=_decode_scratch(n_q),
    )
    return pl.pallas_call(
        functools.partial(_attn_decode_kernel, pages=pages, n_q=n_q, lam_init=lam_init),
        grid_spec=grid_spec,
        out_shape=jax.ShapeDtypeStruct((b, n_q, hw), BF16),
        compiler_params=_cparams("parallel", "arbitrary"),
        name="attn_decode",
    )(pt, lam_p, subln_g.reshape(1, ATTN_DV), qrows, kn, vn, *([ck] * pages), *([cv] * pages))


def _ssm_in_kernel(h_ref, g_ref, w_ref, wdt_ref, z_ref, xbc_ref, dt_ref):
    xn = _rms(h_ref[...], g_ref[...]).astype(BF16)
    z_ref[...] = _dot(xn, w_ref[:, :SSM_INNER])
    xbc_ref[...] = _dot(xn, w_ref[:, SSM_INNER:SSM_INNER + SSM_CONV_DIM])
    dt_ref[...] = _dot(xn, wdt_ref[...])


def ssm_in_proj(h, g_mix, w_in_all, layer, w_dt, *, tile=512):
    m, d = h.shape
    tm = _row_tile(m, tile)
    row = lambda w: pl.BlockSpec((tm, w), lambda i: (i, 0))
    return pl.pallas_call(
        _ssm_in_kernel,
        grid=(m // tm,),
        in_specs=[row(d), _const_spec((1, d)), _layer_spec(w_in_all, layer), _const_spec((d, LANE))],
        out_specs=[row(SSM_INNER), row(SSM_CONV_DIM), row(LANE)],
        out_shape=[jax.ShapeDtypeStruct((m, SSM_INNER), F32),
                   jax.ShapeDtypeStruct((m, SSM_CONV_DIM), F32),
                   jax.ShapeDtypeStruct((m, LANE), F32)],
        compiler_params=_cparams("parallel"),
        name="ssm_in_proj",
    )(h, g_mix.reshape(1, d), w_in_all, w_dt)


def _split3(x):
    x1 = x.astype(BF16)
    r = x - x1.astype(F32)
    x2 = r.astype(BF16)
    x3 = (r - x2.astype(F32)).astype(BF16)
    return x1, x2, x3


def _spread_dot(x, sel):
    x1 = x.astype(BF16)
    x2 = (x - x1.astype(F32)).astype(BF16)
    return _dot(x1, sel) + _dot(x2, sel)


def _exact_dot_left(sel, x):
    x1, x2, x3 = _split3(x)
    return _dot(sel, x1) + _dot(sel, x2) + _dot(sel, x3)


N_SSD_IN = 12


def _ssd_body(z_ref, xbc_ref, dt_ref, cbuf_ref, h0_ref, cw_ref, cb_ref, dtb_ref, alog_ref,
              dskip_ref, ng_ref, spread_ref, y_ref, hf_ref, state_ref, tail_ref, *, valid,
              state_out_every_step=False):
    c = pl.program_id(1)
    q = xbc_ref.shape[0]

    @pl.when(c == 0)
    def _():
        state_ref[...] = h0_ref[...]
        tail_ref[...] = cbuf_ref[...]

    cur = xbc_ref[...]
    prev = tail_ref[...]
    sub = lax.broadcasted_iota(jnp.int32, prev.shape, 0)
    conv = cb_ref[...] + cur * cw_ref[SSM_CONV - 1:SSM_CONV, :]
    for tap in range(SSM_CONV - 1):
        shift = SSM_CONV - 1 - tap
        rolled = pltpu.roll(cur, shift, 0)
        head = jnp.where(sub < shift, pltpu.roll(prev, shift, 0), rolled[0:8])
        shifted = jnp.concatenate([head, rolled[8:]], axis=0)
        conv = conv + shifted * cw_ref[tap:tap + 1, :]
    tail_ref[...] = cur[q - 8:q]
    conv = conv * jax.nn.sigmoid(conv)
    xs = conv[:, :SSM_INNER]
    bm = conv[:, SSM_INNER:SSM_INNER + SSM_GN].astype(BF16)
    cm = conv[:, SSM_INNER + SSM_GN:].astype(BF16)

    row = lax.broadcasted_iota(jnp.int32, (q, q), 0)
    col = lax.broadcasted_iota(jnp.int32, (q, q), 1)
    tri = col <= row
    dt = jax.nn.softplus(dt_ref[...] + dtb_ref[...])
    if valid < q:
        dt = jnp.where(lax.broadcasted_iota(jnp.int32, dt.shape, 0) < valid, dt, 0.0)
    a = dt * (-jnp.exp(alog_ref[...]))
    acum = _exact_dot_left(tri.astype(BF16), a)
    acum_t = acum.T
    spread = spread_ref[...]
    dt_x = _spread_dot(dt, spread)
    grow_x = _spread_dot(jnp.exp(acum), spread)
    wst_x = _spread_dot(jnp.exp(acum[q - 1:q, :] - acum), spread)
    chunk_decay = jnp.exp(acum_t[:, q - 1:q])

    xd = xs * dt_x
    xd_b = xd.astype(BF16)
    xw_b = (xd * wst_x).astype(BF16)
    y_parts = []
    for g in range(SSM_GROUPS):
        ncols = slice(g * SSM_STATE, (g + 1) * SSM_STATE)
        cb = _dot_nt(cm[:, ncols], bm[:, ncols])
        heads = range(g * SSM_HPG, (g + 1) * SSM_HPG)
        s_g = jnp.concatenate([state_ref[h] for h in heads], axis=0)
        gcols = slice(g * SSM_HPG * SSM_HEAD_DIM, (g + 1) * SSM_HPG * SSM_HEAD_DIM)
        y_off = _dot_nt(cm[:, ncols], s_g.astype(BF16)) * grow_x[:, gcols]
        y_diag = []
        for h in heads:
            seg = acum[:, h:h + 1] - acum_t[h:h + 1, :]
            lmat = (cb * jnp.exp(jnp.where(tri, seg, -jnp.inf))).astype(BF16)
            y_diag.append(_dot(lmat, xd_b[:, h * SSM_HEAD_DIM:(h + 1) * SSM_HEAD_DIM]))
        y_parts.append(jnp.concatenate(y_diag, axis=1) + y_off)
        new_states = _dot_tn(xw_b[:, gcols], bm[:, ncols])
        for i, h in enumerate(heads):
            state_ref[h] = state_ref[h] * chunk_decay[h:h + 1, :] + new_states[i * SSM_HEAD_DIM:(i + 1) * SSM_HEAD_DIM, :]
    y = jnp.concatenate(y_parts, axis=1) + xs * dskip_ref[...]
    zf = z_ref[...]
    y = y * (zf * jax.nn.sigmoid(zf))
    gw = SSM_INNER // SSM_GROUPS
    out = []
    for g in range(SSM_GROUPS):
        yg = y[:, g * gw:(g + 1) * gw]
        out.append(_rms(yg, ng_ref[:, g * gw:(g + 1) * gw]))
    y_ref[...] = jnp.concatenate(out, axis=1).astype(BF16)

    if state_out_every_step:
        hf_ref[...] = state_ref[...]
    else:
        @pl.when(c == pl.num_programs(1) - 1)
        def _():
            hf_ref[...] = state_ref[...]


def _ssd_kernel(*refs, valid):
    _ssd_body(*refs, valid=valid)


def _ssd_decode_kernel(pt_ref, *refs, valid, pages, n_q, lam_init, steps_per_seq):
    ssd_in, rest = refs[:N_SSD_IN], refs[N_SSD_IN:]
    lam_ref, g_ref, q_ref, kn_ref, vn_ref = rest[:5]
    k_refs, v_refs = rest[5:5 + pages], rest[5 + pages:5 + 2 * pages]
    y_ref, hf_ref, o_ref, state_ref, tail_ref, m_ref, l_ref, acc_ref = rest[5 + 2 * pages:]
    t = pl.program_id(0) * pl.num_programs(1) + pl.program_id(1)
    pl.when(t == 0)(functools.partial(_zero_decode_stats, m_ref, l_ref, acc_ref))
    _ssd_body(*ssd_in, y_ref, hf_ref, state_ref, tail_ref, valid=valid, state_out_every_step=True)
    part = t % steps_per_seq
    _decode_step(lam_ref, g_ref, q_ref, kn_ref, vn_ref, k_refs, v_refs, o_ref, m_ref, l_ref, acc_ref,
                 first=part == 0, last=part == steps_per_seq - 1, n_q=n_q, lam_init=lam_init)


def ssd_scan(z, xbc, dt_raw, conv_buf, h0, conv_w, conv_b, dt_bias, a_log, d_skip, norm_g, *,
             batch, chunk_len, valid, decode=None):
    m = z.shape[0]
    seq = m // batch
    q = chunk_len
    assert seq % q == 0 and (valid == q or seq == q)
    nc = seq // q
    z3 = z.reshape(batch, seq, SSM_INNER)
    x3 = xbc.reshape(batch, seq, SSM_CONV_DIM)
    d3 = dt_raw.reshape(batch, seq, LANE)
    cbuf = jnp.pad(conv_buf, ((0, 0), (8 - (SSM_CONV - 1), 0), (0, 0)))
    spread = (jnp.arange(SSM_INNER)[None, :] // SSM_HEAD_DIM == jnp.arange(LANE)[:, None]).astype(BF16)
    lane_pad = lambda v: jnp.pad(v.reshape(1, -1), ((0, 0), (0, LANE - v.shape[-1])))
    chunk = lambda w: pl.BlockSpec((None, q, w), lambda b, c, *_: (b, c, 0))
    state_spec = pl.BlockSpec((None, SSM_HEADS, SSM_HEAD_DIM, SSM_STATE), lambda b, c, *_: (b, 0, 0, 0))
    const = lambda shape: pl.BlockSpec(shape, lambda b, c, *_: (0,) * len(shape))
    in_specs = [chunk(SSM_INNER), chunk(SSM_CONV_DIM), chunk(LANE),
                pl.BlockSpec((None, 8, SSM_CONV_DIM), lambda b, c, *_: (b, 0, 0)), state_spec,
                const((SSM_CONV, SSM_CONV_DIM)), const((1, SSM_CONV_DIM)), const((1, LANE)),
                const((1, LANE)), const((1, SSM_INNER)), const((1, SSM_INNER)),
                const((LANE, SSM_INNER))]
    assert len(in_specs) == N_SSD_IN
    operands = (z3, x3, d3, cbuf, h0, conv_w, conv_b.reshape(1, -1), lane_pad(dt_bias), lane_pad(a_log),
                jnp.repeat(d_skip, SSM_HEAD_DIM).reshape(1, -1), norm_g.reshape(1, -1), spread)
    out_specs = [chunk(SSM_INNER), state_spec]
    out_shape = [jax.ShapeDtypeStruct((batch, seq, SSM_INNER), BF16), jax.ShapeDtypeStruct(h0.shape, F32)]
    scratch = [pltpu.VMEM((SSM_HEADS, SSM_HEAD_DIM, SSM_STATE), F32), pltpu.VMEM((8, SSM_CONV_DIM), F32)]
    if decode is None:
        y, hf = pl.pallas_call(
            functools.partial(_ssd_kernel, valid=valid),
            grid=(batch, nc), in_specs=in_specs, out_specs=out_specs, out_shape=out_shape,
            scratch_shapes=scratch, compiler_params=_cparams("parallel", "arbitrary"), name="ssd_scan",
        )(*operands)
        return y.reshape(m, SSM_INNER), hf

    b_dec, n_q, hw = decode["qb"].shape
    n_pages = decode["page_table"].shape[1]
    steps_per_seq, pages = _decode_split(batch * nc, b_dec, n_pages)
    pt, qrows, kn, vn, ck, cv = _decode_operands(decode["qb"], decode["kb_new"], decode["vb_new"],
                                                 decode["cache_k"], decode["cache_v"],
                                                 decode["page_table"], decode["layer"])
    n_rows = qrows.shape[1]
    step_of = lambda b, c: b * nc + c

    def page_spec(i):
        return pl.BlockSpec((None,) + ck.shape[1:], lambda b, c, pt_ref: (pt_ref[step_of(b, c) * pages + i], 0, 0))

    per_seq = lambda r, w: pl.BlockSpec((None, r, w), lambda b, c, pt_ref: (step_of(b, c) // steps_per_seq, 0, 0))
    dec_specs = [const((4, ATTN_DK)), const((1, ATTN_DV)), per_seq(n_rows, ATTN_HD), per_seq(NEW_ROWS, hw),
                 per_seq(NEW_ROWS, hw)] + [page_spec(i) for i in range(pages)] * 2
    grid_spec = pltpu.PrefetchScalarGridSpec(
        num_scalar_prefetch=1, grid=(batch, nc),
        in_specs=in_specs + dec_specs,
        out_specs=out_specs + [per_seq(n_q, hw)],
        scratch_shapes=scratch + _decode_scratch(n_q),
    )
    y, hf, o_dec = pl.pallas_call(
        functools.partial(_ssd_decode_kernel, valid=valid, pages=pages, n_q=n_q,
                          lam_init=decode["lam_init"], steps_per_seq=steps_per_seq),
        grid_spec=grid_spec,
        out_shape=out_shape + [jax.ShapeDtypeStruct((b_dec, n_q, hw), BF16)],
        compiler_params=_cparams("arbitrary", "arbitrary"),
        name="ssd_scan_decode",
    )(pt, *operands, decode["lam_p"], decode["subln_g"].reshape(1, ATTN_DV), qrows, kn, vn,
      *([ck] * pages), *([cv] * pages))
    return y.reshape(m, SSM_INNER), hf, o_dec


def _ssm_group(h, batch, seq, conv_buf, h0, g_mix, w_in_all, layer, w_dt, conv_w, conv_b, dt_bias, a_log,
               d_skip, norm_g, decode=None):
    d = h.shape[1]
    if seq % SSM_CHUNK == 0:
        padded = seq
        chunk_len = valid = SSM_CHUNK
        hin = h
    else:
        assert seq <= SSM_SHORT_CHUNK
        padded = chunk_len = SSM_SHORT_CHUNK
        valid = seq
        hin = jnp.pad(h.reshape(batch, seq, d), ((0, 0), (0, padded - seq), (0, 0))).reshape(-1, d)
    z, xbc, dt = ssm_in_proj(hin, g_mix, w_in_all, layer, w_dt)
    y, hf, *dec_out = ssd_scan(z, xbc, dt, conv_buf, h0, conv_w, conv_b, dt_bias, a_log, d_skip, norm_g,
                               batch=batch, chunk_len=chunk_len, valid=valid, decode=decode)
    xbc3 = xbc.reshape(batch, padded, SSM_CONV_DIM)[:, :seq]
    new_buf = jnp.concatenate([conv_buf, xbc3], axis=1)[:, seq:]
    y = y.reshape(batch, padded, SSM_INNER)[:, :seq].reshape(batch * seq, SSM_INNER)
    return (y, new_buf, hf, *dec_out)


def kernel(x_prompt, x_sample, cache_k, cache_v, state_ssm, state_conv, page_table, p_prompt, p_sample, norm_mix_g, norm_mlp_g, norm_ple_g, mlp_w_up, mlp_w_down, ple_w_gate, ple_w_proj, final_norm_g, sgu_w_in, sgu_ln_g, sgu_ln_b, sgu_w_s, sgu_b_s, sgu_w_out, attn_w_qkv, attn_lambda, attn_subln_g, attn_w_out, ssm_w_in, ssm_conv_w, ssm_conv_b, ssm_dt_bias, ssm_a_log, ssm_d, ssm_norm_g, ssm_w_out):
    b_p, l_p, d = x_prompt.shape
    b_s, l_s, _ = x_sample.shape
    depth = norm_mix_g.shape[0]
    past_len = page_table.shape[1] * cache_k.shape[2]
    m_s = b_s * l_s
    assert m_s % SGU_CHUNK == 0 and SGU_CHUNK % l_s == 0 and l_p % SGU_CHUNK == 0
    bf = lambda w: w.astype(BF16)

    hp = x_prompt.reshape(b_p * l_p, d)
    hs = x_sample.reshape(m_s, d)
    rope_p = _rope_tables(jnp.arange(l_p, dtype=jnp.int32))
    rope_s = _rope_tables(past_len + jnp.arange(l_s, dtype=jnp.int32))

    w_up, w_down, w_gate, w_proj = bf(mlp_w_up), bf(mlp_w_down), bf(ple_w_gate), bf(ple_w_proj)
    w_sgu_in, w_sgu_out = bf(sgu_w_in), bf(sgu_w_out)
    w_qkv, w_attn_out = bf(attn_w_qkv), bf(attn_w_out)
    w_ssm_in, w_ssm_out = bf(ssm_w_in), bf(ssm_w_out)
    pp = p_prompt.reshape(depth, b_p * l_p, D_PLE)
    ps = p_sample.reshape(depth, m_s, D_PLE)

    k_p, v_p, k_s, v_s = [], [], [], []
    ssm_p, conv_p, ssm_s, conv_s = [], [], [], []
    sgu_s = []

    def channel(h, mix, p_all, i, w_mix, j, tile):
        return channel_update(h, mix, p_all, i, w_mix, j, norm_mlp_g[i], w_up, w_down, norm_ple_g[i],
                              w_gate, w_proj, final_norm_g, final_norm=i == depth - 1, tile=tile)

    def decode_rides_along(i):
        if i + 1 >= depth or (i + 1) % N_MIXERS != 2 or l_p % SSM_CHUNK:
            return False
        return _decode_split(b_p * (l_p // SSM_CHUNK), b_s, page_table.shape[1]) is not None

    pending = None
    for i in range(depth):
        kind, j = i % N_MIXERS, i // N_MIXERS
        g_mix = norm_mix_g[i]
        if kind == 0:
            ws_p, bs_p = _sgu_spatial(sgu_w_s[j], sgu_b_s[j], l_p)
            ws_s, bs_s = _sgu_spatial(sgu_w_s[j], sgu_b_s[j], l_s)
            mix_p, _ = sgu_mix(hp, g_mix, w_sgu_in, j, sgu_ln_g[j], sgu_ln_b[j], ws_p, bs_p, want_v=False)
            mix_s, v_rows = sgu_mix(hs, g_mix, w_sgu_in, j, sgu_ln_g[j], sgu_ln_b[j], ws_s, bs_s,
                                    want_v=True, tile=SGU_CHUNK)
            sgu_s.append(v_rows.reshape(b_s, l_s, SGU_WIDTH))
            w_mix = w_sgu_out
        elif kind == 1:
            lam_init = 0.8 - 0.6 * math.exp(-0.3 * i)
            qb, k, v, kb, vb = qkv_rope(hp, g_mix, w_qkv, j, *rope_p, transposed_v=True)
            mix_p = attn_prompt(qb, kb, vb, attn_lambda[j], attn_subln_g[j], batch=b_p, lam_init=lam_init)
            k_p.append(k.reshape(b_p, l_p, ATTN_HEADS, ATTN_HD))
            v_p.append(v.reshape(b_p, l_p, ATTN_HEADS, ATTN_DV))
            qb, k, v, kb, vb = qkv_rope(hs, g_mix, w_qkv, j, *rope_s, tile=SGU_CHUNK)
            k_s.append(k.reshape(b_s, l_s, ATTN_HEADS, ATTN_HD))
            v_s.append(v.reshape(b_s, l_s, ATTN_HEADS, ATTN_DV))
            seq3 = lambda x: x.reshape(b_s, l_s, -1)
            dec = dict(qb=seq3(qb), kb_new=seq3(kb), vb_new=seq3(vb), cache_k=cache_k, cache_v=cache_v,
                       page_table=page_table, layer=j, lam_p=attn_lambda[j], subln_g=attn_subln_g[j],
                       lam_init=lam_init)
            w_mix = w_attn_out
            if decode_rides_along(i):
                pending = (dec, (ps, i, w_mix, j))
                mix_s = None
            else:
                mix_s = attn_decode(**dec).reshape(m_s, -1)
        else:
            w_dt = jnp.pad(w_ssm_in[j][:, SSM_INNER + SSM_CONV_DIM:], ((0, 0), (0, LANE - SSM_HEADS)))
            shared = (g_mix, w_ssm_in, j, w_dt, ssm_conv_w[j], ssm_conv_b[j], ssm_dt_bias[j],
                      ssm_a_log[j], ssm_d[j], ssm_norm_g[j])
            zero_buf = jnp.zeros((b_p, SSM_CONV - 1, SSM_CONV_DIM), F32)
            zero_h = jnp.zeros((b_p, SSM_HEADS, SSM_HEAD_DIM, SSM_STATE), F32)
            if pending is None:
                mix_p, cb_p, hf_p = _ssm_group(hp, b_p, l_p, zero_buf, zero_h, *shared)
            else:
                dec, prev_channel = pending
                pending = None
                mix_p, cb_p, hf_p, o_dec = _ssm_group(hp, b_p, l_p, zero_buf, zero_h, *shared, decode=dec)
                hs = channel(hs, o_dec.reshape(m_s, -1), *prev_channel, SGU_CHUNK)
            mix_s, cb_s, hf_s = _ssm_group(hs, b_s, l_s, state_conv[j], state_ssm[j], *shared)
            ssm_p.append(hf_p)
            conv_p.append(cb_p)
            ssm_s.append(hf_s)
            conv_s.append(cb_s)
            w_mix = w_ssm_out
        hp = channel(hp, mix_p, pp, i, w_mix, j, 512)
        if mix_s is not None:
            hs = channel(hs, mix_s, ps, i, w_mix, j, SGU_CHUNK)

    return (hp.reshape(b_p, l_p, d), hs.reshape(b_s, l_s, d),
            jnp.stack(k_p), jnp.stack(v_p), jnp.stack(k_s), jnp.stack(v_s),
            jnp.stack(ssm_p), jnp.stack(conv_p), jnp.stack(ssm_s), jnp.stack(conv_s),
            jnp.stack(sgu_s))
```

```python
import functools
import math

import jax
import jax.numpy as jnp
from jax import lax
from jax.experimental import pallas as pl
from jax.experimental.pallas import tpu as pltpu

F32 = jnp.float32
BF16 = jnp.bfloat16

EPS = 1e-6
D_MODEL = 1024
D_FF = 4 * D_MODEL
D_PLE = 256
N_MIXERS = 3

SGU_CHUNK = 128
SGU_WIDTH = 2 * D_MODEL
SGU_GROUPS = 8
SGU_GDIM = SGU_WIDTH // SGU_GROUPS

ATTN_HEADS = 8
ATTN_DK = 64
ATTN_DV = 128
ATTN_HD = 2 * ATTN_DK
ROPE_THETA = 10000.0

SSM_INNER = 2 * D_MODEL
SSM_HEAD_DIM = 64
SSM_HEADS = SSM_INNER // SSM_HEAD_DIM
SSM_GROUPS = 4
SSM_STATE = 128
SSM_CONV = 4
SSM_CHUNK = 128
SSM_SHORT_CHUNK = 16
SSM_GN = SSM_GROUPS * SSM_STATE
SSM_CONV_DIM = SSM_INNER + 2 * SSM_GN
SSM_HPG = SSM_HEADS // SSM_GROUPS

VMEM_LIMIT_BYTES = 56 * 1024 * 1024
LANE = 128
SUBLANE = 8

ROW_TILE = 512
SGU_ROW_TILE = 256
ATTN_TILE = 1024
SMALL_TILE = SGU_CHUNK


def _cparams(*sem):
    return pltpu.CompilerParams(dimension_semantics=sem, vmem_limit_bytes=VMEM_LIMIT_BYTES)


def _const_spec(shape):
    zeros = (0,) * len(shape)
    return pl.BlockSpec(shape, lambda *_: zeros, pipeline_mode=pl.Buffered(1))


def _layer_spec(stack, layer):
    tail = tuple(stack.shape[1:])
    zeros = (0,) * len(tail)
    return pl.BlockSpec((None,) + tail, lambda *_: (layer,) + zeros, pipeline_mode=pl.Buffered(1))


def _rms(x, g):
    ms = jnp.mean(x * x, axis=-1, keepdims=True)
    return (x * lax.rsqrt(ms + EPS)) * g


def _dot(a, b):
    return jnp.dot(a, b, preferred_element_type=F32)


def _dot_nt(a, b):
    return lax.dot_general(a, b, (((1,), (1,)), ((), ())), preferred_element_type=F32)


def _dot_tn(a, b):
    return lax.dot_general(a, b, (((0,), (0,)), ((), ())), preferred_element_type=F32)


def _row_tile(m, want):
    t = min(m, want)
    assert m % t == 0, (m, t)
    return t


def _channel_kernel(h_ref, mix_ref, p_ref, wmix_ref, gmlp_ref, wup_ref, wdown_ref, gple_ref,
                    wgate_ref, wproj_ref, gfin_ref, out_ref, *, final_norm):
    h = h_ref[...] + _dot(mix_ref[...], wmix_ref[...])
    xn = _rms(h, gmlp_ref[...]).astype(BF16)
    a = jnp.square(jnp.maximum(_dot(xn, wup_ref[...]), 0.0)).astype(BF16)
    h = h + _dot(a, wdown_ref[...])
    xn = _rms(h, gple_ref[...]).astype(BF16)
    gate = jax.nn.sigmoid(_dot(xn, wgate_ref[...]))
    h = h + gate * _dot(p_ref[...].astype(BF16), wproj_ref[...])
    if final_norm:
        h = _rms(h, gfin_ref[...])
    out_ref[...] = h


def channel_update(h, mix, p_all, layer, w_mix_all, mix_layer, g_mlp, w_up_all, w_down_all, g_ple,
                   w_gate_all, w_proj_all, g_fin, *, final_norm, tile=ROW_TILE):
    m, d = h.shape
    dm = mix.shape[1]
    tm = _row_tile(m, tile)
    row = lambda w: pl.BlockSpec((tm, w), lambda i: (i, 0))
    return pl.pallas_call(
        functools.partial(_channel_kernel, final_norm=final_norm),
        grid=(m // tm,),
        in_specs=[row(d), row(dm), pl.BlockSpec((None, tm, D_PLE), lambda i: (layer, i, 0)),
                  _layer_spec(w_mix_all, mix_layer), _const_spec((1, d)),
                  _layer_spec(w_up_all, layer), _layer_spec(w_down_all, layer), _const_spec((1, d)),
                  _layer_spec(w_gate_all, layer), _layer_spec(w_proj_all, layer), _const_spec((1, d))],
        out_specs=row(d),
        out_shape=jax.ShapeDtypeStruct((m, d), F32),
        compiler_params=_cparams("parallel"),
        name="channel_update",
    )(h, mix, p_all, w_mix_all, g_mlp.reshape(1, d), w_up_all, w_down_all, g_ple.reshape(1, d),
      w_gate_all, w_proj_all, g_fin.reshape(1, d))


def _sgu_kernel(h_ref, g_ref, win_ref, lng_ref, lnb_ref, ws_ref, bs_ref, mix_ref, *v_out,
                n_chunks):
    xn = _rms(h_ref[...], g_ref[...]).astype(BF16)
    v = jax.nn.gelu(_dot(xn, win_ref[:, SGU_WIDTH:]))
    u = jax.nn.gelu(_dot(xn, win_ref[:, :SGU_WIDTH]))
    mu = jnp.mean(v, axis=-1, keepdims=True)
    vc = v - mu
    v = (vc * lax.rsqrt(jnp.mean(vc * vc, axis=-1, keepdims=True) + EPS)) * lng_ref[...] + lnb_ref[...]
    if v_out:
        v_out[0][...] = v
    vb = v.astype(BF16)
    bs = bs_ref[...]
    for c in range(n_chunks):
        rows = slice(c * SGU_CHUNK, (c + 1) * SGU_CHUNK)
        for g in range(SGU_GROUPS):
            cols = slice(g * SGU_GDIM, (g + 1) * SGU_GDIM)
            s = _dot(ws_ref[g], vb[rows, cols]) + bs[:, g:g + 1]
            mix_ref[rows, cols] = (u[rows, cols] * s).astype(BF16)


def _sgu_spatial(w_s, b_s, seq):
    q = min(SGU_CHUNK, seq)
    tri = jnp.tril(jnp.ones((q, q), dtype=bool))
    ws = jnp.where(tri, w_s[:, :q, :q], 0)
    bs = b_s[:, :q]
    rep = SGU_CHUNK // q
    if rep > 1:
        eye = jnp.eye(rep, dtype=ws.dtype)
        ws = jnp.einsum("ab,gts->gatbs", eye, ws).reshape(SGU_GROUPS, SGU_CHUNK, SGU_CHUNK)
        bs = jnp.tile(bs, (1, rep))
    return ws.astype(BF16), bs.T


def sgu_mix(h, g_mix, w_in_all, layer, ln_g, ln_b, ws, bs, *, want_v, tile=SGU_ROW_TILE):
    m, d = h.shape
    tm = _row_tile(m, tile)
    row = lambda w: pl.BlockSpec((tm, w), lambda i: (i, 0))
    out_shape = [jax.ShapeDtypeStruct((m, SGU_WIDTH), BF16)]
    out_specs = [row(SGU_WIDTH)]
    if want_v:
        out_shape.append(jax.ShapeDtypeStruct((m, SGU_WIDTH), F32))
        out_specs.append(row(SGU_WIDTH))
    outs = pl.pallas_call(
        functools.partial(_sgu_kernel, n_chunks=tm // SGU_CHUNK),
        grid=(m // tm,),
        in_specs=[row(d), _const_spec((1, d)), _layer_spec(w_in_all, layer),
                  _const_spec((1, SGU_WIDTH)), _const_spec((1, SGU_WIDTH)),
                  _const_spec((SGU_GROUPS, SGU_CHUNK, SGU_CHUNK)),
                  _const_spec((SGU_CHUNK, SGU_GROUPS))],
        out_specs=out_specs,
        out_shape=out_shape,
        compiler_params=_cparams("parallel"),
        name="sgu_mix",
    )(h, g_mix.reshape(1, d), w_in_all, ln_g.reshape(1, -1), ln_b.reshape(1, -1), ws, bs)
    return outs if want_v else (outs[0], None)


def _rope_tables(pos):
    half = ATTN_DK // 2
    inv = ROPE_THETA ** (-jnp.arange(half, dtype=F32) / half)
    ang = pos.astype(F32)[:, None] * inv[None, :]
    cos, sin = jnp.cos(ang), jnp.sin(ang)
    cos_t = jnp.tile(cos, (1, 4))
    sin_t = jnp.tile(jnp.concatenate([-sin, sin], axis=1), (1, 2))
    return cos_t, sin_t


def _rope_head(x, cos, sin_signed, first_half):
    partner = jnp.where(first_half, pltpu.roll(x, LANE - ATTN_DK // 2, 1),
                        pltpu.roll(x, ATTN_DK // 2, 1))
    return x * cos + partner * sin_signed


def _qkv_kernel(h_ref, g_ref, w_ref, cos_ref, sin_ref, qb_ref, k_ref, v_ref, kb_ref, vb_ref, *,
                transposed_v):
    xn = _rms(h_ref[...], g_ref[...]).astype(BF16)
    qkv = _dot(xn, w_ref[...])
    cos = cos_ref[...]
    sin = sin_ref[...]
    lane = lax.broadcasted_iota(jnp.int32, cos.shape, 1)
    first_half = (lane % ATTN_DK) < (ATTN_DK // 2)
    qk_dim = ATTN_HEADS * ATTN_HD
    scale = ATTN_DK ** -0.5 * math.log2(math.e)
    rows = h_ref.shape[0]
    for hd in range(ATTN_HEADS):
        cols = slice(hd * ATTN_HD, (hd + 1) * ATTN_HD)
        q = _rope_head(qkv[:, cols], cos, sin, first_half)
        k = _rope_head(qkv[:, qk_dim + hd * ATTN_HD:qk_dim + (hd + 1) * ATTN_HD], cos, sin,
                       first_half)
        qb_ref[:, cols] = (q * scale).astype(BF16)
        k_ref[pl.ds(hd, rows, stride=ATTN_HEADS), :] = k
        kb_ref[:, cols] = k.astype(BF16)
        v_ref[pl.ds(hd, rows, stride=ATTN_HEADS), :] = qkv[:, 2 * qk_dim + hd * ATTN_DV:
                                                           2 * qk_dim + (hd + 1) * ATTN_DV]
    v = qkv[:, 2 * qk_dim:]
    vb_ref[...] = (v.T if transposed_v else v).astype(BF16)


def qkv_rope(h, g_mix, w_qkv_all, layer, cos_t, sin_t, *, transposed_v=False, tile=ROW_TILE):
    m, d = h.shape
    seq = cos_t.shape[0]
    tm = _row_tile(m, tile)
    row = lambda w: pl.BlockSpec((tm, w), lambda i: (i, 0))
    if seq % tm == 0:
        n_blk = seq // tm
        table = pl.BlockSpec((tm, LANE), lambda i: (i % n_blk, 0))
    else:
        assert tm % seq == 0, (tm, seq)
        cos_t = jnp.tile(cos_t, (tm // seq, 1))
        sin_t = jnp.tile(sin_t, (tm // seq, 1))
        table = _const_spec((tm, LANE))
    hw = ATTN_HEADS * ATTN_HD
    vb_spec, vb_shape = row(hw), jax.ShapeDtypeStruct((m, hw), BF16)
    head_rows = pl.BlockSpec((tm * ATTN_HEADS, ATTN_HD), lambda i: (i, 0))
    if transposed_v:
        assert seq % tm == 0
        n_blk = seq // tm
        vb_spec = pl.BlockSpec((None, hw, tm), lambda i: (i // n_blk, 0, i % n_blk))
        vb_shape = jax.ShapeDtypeStruct((m // seq, hw, seq), BF16)
    return pl.pallas_call(
        functools.partial(_qkv_kernel, transposed_v=transposed_v),
        grid=(m // tm,),
        in_specs=[row(d), _const_spec((1, d)), _layer_spec(w_qkv_all, layer), table, table],
        out_specs=[row(hw), head_rows, head_rows, row(hw), vb_spec],
        out_shape=[jax.ShapeDtypeStruct((m, hw), BF16),
                   jax.ShapeDtypeStruct((m * ATTN_HEADS, ATTN_HD), F32),
                   jax.ShapeDtypeStruct((m * ATTN_HEADS, ATTN_DV), F32),
                   jax.ShapeDtypeStruct((m, hw), BF16), vb_shape],
        compiler_params=_cparams("parallel"),
        name="qkv_rope",
    )(h, g_mix.reshape(1, d), w_qkv_all, cos_t, sin_t)


def _lambda_value(lam_ref, lam_init):
    lp = lam_ref[...]
    s1 = jnp.sum(lp[0:1] * lp[1:2], axis=-1, keepdims=True)
    s2 = jnp.sum(lp[2:3] * lp[3:4], axis=-1, keepdims=True)
    return jnp.exp(s1) - jnp.exp(s2) + lam_init


def _softmax_step_t(s, v, m_prev, acc_prev):
    m_new = jnp.maximum(m_prev, jnp.max(s, axis=0, keepdims=True))
    alpha = jnp.exp2(m_prev - m_new)
    p = jnp.exp2(s - m_new)
    acc = alpha * acc_prev + _dot(v, p.astype(BF16))
    return m_new, acc


def _attn_prompt_kernel(lam_ref, g_ref, q_ref, k_ref, v_ref, o_ref, *, tile, n_tiles, lam_init):
    qi = pl.program_id(2)
    half = tile // 2
    q = q_ref[...]
    lane = lax.broadcasted_iota(jnp.int32, q.shape, 1)
    zero = jnp.zeros_like(q)
    q1 = jnp.where(lane < ATTN_DK, q, zero)
    q2 = jnp.where(lane < ATTN_DK, zero, q)

    def block(start, n_keys, q_lo, carry, masked):
        k = k_ref[pl.ds(start, n_keys), :]
        ones_row = (lax.broadcasted_iota(jnp.int32, (ACC_PAD, n_keys), 0) == 0).astype(BF16)
        v = jnp.concatenate([v_ref[:, pl.ds(start, n_keys)], ones_row], axis=0)
        s1 = _dot_nt(k, q1[q_lo:])
        s2 = _dot_nt(k, q2[q_lo:])
        if masked:
            kr = lax.broadcasted_iota(jnp.int32, s1.shape, 0)
            qc = lax.broadcasted_iota(jnp.int32, s1.shape, 1)
            s1 = jnp.where(kr <= qc, s1, -jnp.inf)
            s2 = jnp.where(kr <= qc, s2, -jnp.inf)
        c1, c2 = carry
        return _softmax_step_t(s1, v, *c1), _softmax_step_t(s2, v, *c2)

    init = (jnp.full((1, tile), -jnp.inf, F32), jnp.zeros((ATTN_DV + ACC_PAD, tile), F32))
    lam = _lambda_value(lam_ref, lam_init)

    def tile_at(n_full):
        carry = (init, init)
        for ki in range(n_full):
            carry = block(ki * tile, tile, 0, carry, False)
        base = n_full * tile
        carry = block(base, half, 0, carry, True)
        tail = tuple(tuple(x[:, half:] for x in c) for c in carry)
        tail = block(base + half, half, half, tail, True)
        (_, a1), (_, a2) = tuple(
            tuple(jnp.concatenate([x[:, :half], y], axis=1) for x, y in zip(c, t)) for c, t in zip(carry, tail))
        normed = lambda a: a[:ATTN_DV] / a[ATTN_DV:ATTN_DV + 1]
        o = (normed(a1) - lam * normed(a2)).T
        o = _rms(o, g_ref[...]) * (1.0 - lam_init)
        o_ref[...] = o.astype(BF16)

    for n_full in range(n_tiles):
        pl.when(qi == n_full)(functools.partial(tile_at, n_full))


def attn_prompt(qb, kb, vb_t, lam_p, subln_g, *, batch, lam_init, tile=ATTN_TILE):
    m, hw = qb.shape
    seq = m // batch
    t = _row_tile(seq, tile)
    q3, k3 = (x.reshape(batch, seq, hw) for x in (qb, kb))
    q_spec = pl.BlockSpec((None, t, ATTN_HD), lambda b, h, qi: (b, qi, h))
    k_spec = pl.BlockSpec((None, seq, ATTN_HD), lambda b, h, qi: (b, 0, h))
    vt_spec = pl.BlockSpec((None, ATTN_DV, seq), lambda b, h, qi: (b, h, 0))
    const = lambda shape: pl.BlockSpec(shape, lambda b, h, qi: (0,) * len(shape))
    out = pl.pallas_call(
        functools.partial(_attn_prompt_kernel, tile=t, n_tiles=seq // t, lam_init=lam_init),
        grid=(batch, ATTN_HEADS, seq // t),
        in_specs=[const((4, ATTN_DK)), const((1, ATTN_DV)), q_spec, k_spec, vt_spec],
        out_specs=q_spec,
        out_shape=jax.ShapeDtypeStruct((batch, seq, hw), BF16),
        compiler_params=_cparams("parallel", "parallel", "parallel"),
        name="attn_prompt",
    )(lam_p, subln_g.reshape(1, ATTN_DV), q3, k3, vb_t)
    return out.reshape(m, hw)


NEW_ROWS = 16
ACC_PAD = 16


def _decode_step(lam_ref, g_ref, q_ref, kn_ref, vn_ref, k_refs, v_refs, o_ref, m_ref, l_ref, acc_ref, *,
                 first, last, n_q, lam_init):
    rows = 2 * n_q
    q_all = q_ref[...].astype(BF16)

    def update(s, pv_fn, fresh):
        m_prev = jnp.where(fresh, -jnp.inf, m_ref[...])
        m_new = jnp.maximum(m_prev, jnp.max(s, axis=1, keepdims=True))
        alpha = jnp.exp2(m_prev - m_new)
        p = jnp.exp2(s - m_new)
        l_ref[...] = jnp.where(fresh, 0.0, alpha * l_ref[...]) + jnp.sum(p, axis=1, keepdims=True)
        acc_ref[...] = jnp.where(fresh, 0.0, alpha * acc_ref[...]) + pv_fn(p)
        m_ref[...] = m_new

    n_cols = k_refs[0].shape[0]
    r = lax.broadcasted_iota(jnp.int32, (q_all.shape[0], n_cols), 0)
    c = lax.broadcasted_iota(jnp.int32, (q_all.shape[0], n_cols), 1)
    own_head = (c % ATTN_HEADS) == (r // rows)
    s = jnp.concatenate(
        [jnp.where(own_head, _dot_nt(q_all, k_ref[...].astype(BF16)), -jnp.inf) for k_ref in k_refs],
        axis=1)

    def pv_pages(p):
        acc = None
        for i, v_ref in enumerate(v_refs):
            d = _dot(p[:, i * n_cols:(i + 1) * n_cols].astype(BF16), v_ref[...].astype(BF16))
            acc = d if acc is None else acc + d
        return acc

    update(s, pv_pages, first)

    @pl.when(last)
    def _():
        kn = kn_ref[...]
        vn = vn_ref[...]
        head_cols = lambda x, h: x[:, h * ATTN_HD:(h + 1) * ATTN_HD]
        q_heads = [q_all[h * rows:(h + 1) * rows] for h in range(ATTN_HEADS)]
        s_new = jnp.concatenate([_dot_nt(q_heads[h], head_cols(kn, h)) for h in range(ATTN_HEADS)], axis=0)
        rn = lax.broadcasted_iota(jnp.int32, s_new.shape, 0)
        cn = lax.broadcasted_iota(jnp.int32, s_new.shape, 1)
        s_new = jnp.where(cn <= rn % n_q, s_new, -jnp.inf)

        def pv_new(p):
            return jnp.concatenate(
                [_dot(p[h * rows:(h + 1) * rows].astype(BF16), head_cols(vn, h)) for h in range(ATTN_HEADS)],
                axis=0)

        update(s_new, pv_new, False)
        lam = _lambda_value(lam_ref, lam_init)
        o_all = acc_ref[...] / l_ref[...]
        for h in range(ATTN_HEADS):
            r1 = h * rows
            o = o_all[r1:r1 + n_q] - lam * o_all[r1 + n_q:r1 + rows]
            o = _rms(o, g_ref[...]) * (1.0 - lam_init)
            o_ref[:, h * ATTN_DV:(h + 1) * ATTN_DV] = o.astype(BF16)


def _zero_decode_stats(m_ref, l_ref, acc_ref):
    m_ref[...] = jnp.zeros(m_ref.shape, F32)
    l_ref[...] = jnp.zeros(l_ref.shape, F32)
    acc_ref[...] = jnp.zeros(acc_ref.shape, F32)


def _attn_decode_kernel(pt_ref, lam_ref, g_ref, q_ref, kn_ref, vn_ref, *refs, pages, n_q, lam_init):
    o_ref, m_ref, l_ref, acc_ref = refs[2 * pages:]
    step = pl.program_id(1)
    pl.when(step == 0)(functools.partial(_zero_decode_stats, m_ref, l_ref, acc_ref))
    _decode_step(lam_ref, g_ref, q_ref, kn_ref, vn_ref, refs[:pages], refs[pages:2 * pages], o_ref,
                 m_ref, l_ref, acc_ref, first=step == 0, last=step == pl.num_programs(1) - 1,
                 n_q=n_q, lam_init=lam_init)


def _decode_operands(qb, kb_new, vb_new, cache_k, cache_v, page_table, layer):
    b, n_q, hw = qb.shape
    n_layers, n_pool, page, n_heads, hd = cache_k.shape
    assert n_q <= NEW_ROWS and (n_heads, hd) == (ATTN_HEADS, ATTN_HD)
    ck = cache_k.reshape(n_layers * n_pool, page * n_heads, hd)
    cv = cache_v.reshape(n_layers * n_pool, page * n_heads, hd)
    pt = (page_table.astype(jnp.int32) + layer * n_pool).reshape(-1)
    half = (jnp.arange(ATTN_HD) // ATTN_DK)[None, :] == jnp.arange(2)[:, None]
    q4 = qb.reshape(b, n_q, ATTN_HEADS, ATTN_HD).astype(F32)
    qrows = jnp.where(half[None, None, :, None, :], jnp.transpose(q4, (0, 2, 1, 3))[:, :, None], 0.0)
    qrows = qrows.reshape(b, ATTN_HEADS * 2 * n_q, ATTN_HD)
    pad = ((0, 0), (0, NEW_ROWS - n_q), (0, 0))
    return pt, qrows, jnp.pad(kb_new, pad), jnp.pad(vb_new, pad), ck, cv


def _decode_scratch(n_q):
    n_rows = ATTN_HEADS * 2 * n_q
    return [pltpu.VMEM((n_rows, 1), F32), pltpu.VMEM((n_rows, 1), F32), pltpu.VMEM((n_rows, ATTN_DV), F32)]


def _decode_split(steps, n_seq, n_pages):
    if steps % n_seq:
        return None
    steps_per_seq = steps // n_seq
    if n_pages % steps_per_seq:
        return None
    return steps_per_seq, n_pages // steps_per_seq


def attn_decode(qb, kb_new, vb_new, cache_k, cache_v, page_table, layer, lam_p, subln_g, *,
                lam_init, pages=8):
    b, n_q, hw = qb.shape
    n_pages = page_table.shape[1]
    assert n_pages % pages == 0
    pt, qrows, kn, vn, ck, cv = _decode_operands(qb, kb_new, vb_new, cache_k, cache_v, page_table, layer)
    n_rows = qrows.shape[1]

    def page_spec(i):
        return pl.BlockSpec((None,) + ck.shape[1:],
                            lambda bi, s, pt_ref: (pt_ref[bi * n_pages + s * pages + i], 0, 0))

    per_b = lambda r, w: pl.BlockSpec((None, r, w), lambda bi, s, pt_ref: (bi, 0, 0))
    const = lambda shape: pl.BlockSpec(shape, lambda bi, s, pt_ref: (0,) * len(shape))
    grid_spec = pltpu.PrefetchScalarGridSpec(
        num_scalar_prefetch=1,
        grid=(b, n_pages // pages),
        in_specs=[const((4, ATTN_DK)), const((1, ATTN_DV)), per_b(n_rows, ATTN_HD), per_b(NEW_ROWS, hw),
                  per_b(NEW_ROWS, hw)] + [page_spec(i) for i in range(pages)] * 2,
        out_specs=per_b(n_q, hw),
        scratch_shapes=_decode_scratch(n_q),
    )
    return pl.pallas_call(
        functools.partial(_attn_decode_kernel, pages=pages, n_q=n_q, lam_init=lam_init),
        grid_spec=grid_spec,
        out_shape=jax.ShapeDtypeStruct((b, n_q, hw), BF16),
        compiler_params=_cparams("parallel", "arbitrary"),
        name="attn_decode",
    )(pt, lam_p, subln_g.reshape(1, ATTN_DV), qrows, kn, vn, *([ck] * pages), *([cv] * pages))


def _ssm_in_kernel(h_ref, g_ref, w_ref, wdt_ref, z_ref, xbc_ref, dt_ref):
    xn = _rms(h_ref[...], g_ref[...]).astype(BF16)
    z_ref[...] = _dot(xn, w_ref[:, :SSM_INNER])
    xbc_ref[...] = _dot(xn, w_ref[:, SSM_INNER:SSM_INNER + SSM_CONV_DIM])
    dt_ref[...] = _dot(xn, wdt_ref[...])


def ssm_in_proj(h, g_mix, w_in_all, layer, w_dt, *, tile=ROW_TILE):
    m, d = h.shape
    tm = _row_tile(m, tile)
    row = lambda w: pl.BlockSpec((tm, w), lambda i: (i, 0))
    return pl.pallas_call(
        _ssm_in_kernel,
        grid=(m // tm,),
        in_specs=[row(d), _const_spec((1, d)), _layer_spec(w_in_all, layer), _const_spec((d, LANE))],
        out_specs=[row(SSM_INNER), row(SSM_CONV_DIM), row(LANE)],
        out_shape=[jax.ShapeDtypeStruct((m, SSM_INNER), F32),
                   jax.ShapeDtypeStruct((m, SSM_CONV_DIM), F32),
                   jax.ShapeDtypeStruct((m, LANE), F32)],
        compiler_params=_cparams("parallel"),
        name="ssm_in_proj",
    )(h, g_mix.reshape(1, d), w_in_all, w_dt)


def _split3(x):
    x1 = x.astype(BF16)
    r = x - x1.astype(F32)
    x2 = r.astype(BF16)
    x3 = (r - x2.astype(F32)).astype(BF16)
    return x1, x2, x3


def _spread_dot(x, sel):
    x1 = x.astype(BF16)
    x2 = (x - x1.astype(F32)).astype(BF16)
    return _dot(x1, sel) + _dot(x2, sel)


def _exact_dot_left(sel, x):
    x1, x2, x3 = _split3(x)
    return _dot(sel, x1) + _dot(sel, x2) + _dot(sel, x3)


N_SSD_IN = 12


def _ssd_body(z_ref, xbc_ref, dt_ref, cbuf_ref, h0_ref, cw_ref, cb_ref, dtb_ref, alog_ref,
              dskip_ref, ng_ref, spread_ref, y_ref, hf_ref, state_ref, tail_ref, *, valid,
              state_out_every_step=False):
    c = pl.program_id(1)
    q = xbc_ref.shape[0]

    @pl.when(c == 0)
    def _():
        state_ref[...] = h0_ref[...]
        tail_ref[...] = cbuf_ref[...]

    cur = xbc_ref[...]
    prev = tail_ref[...]
    sub = lax.broadcasted_iota(jnp.int32, prev.shape, 0)
    conv = cb_ref[...] + cur * cw_ref[SSM_CONV - 1:SSM_CONV, :]
    for tap in range(SSM_CONV - 1):
        shift = SSM_CONV - 1 - tap
        rolled = pltpu.roll(cur, shift, 0)
        head = jnp.where(sub < shift, pltpu.roll(prev, shift, 0), rolled[0:SUBLANE])
        shifted = jnp.concatenate([head, rolled[SUBLANE:]], axis=0)
        conv = conv + shifted * cw_ref[tap:tap + 1, :]
    tail_ref[...] = cur[q - SUBLANE:q]
    conv = conv * jax.nn.sigmoid(conv)
    xs = conv[:, :SSM_INNER]
    bm = conv[:, SSM_INNER:SSM_INNER + SSM_GN].astype(BF16)
    cm = conv[:, SSM_INNER + SSM_GN:].astype(BF16)

    row = lax.broadcasted_iota(jnp.int32, (q, q), 0)
    col = lax.broadcasted_iota(jnp.int32, (q, q), 1)
    tri = col <= row
    dt = jax.nn.softplus(dt_ref[...] + dtb_ref[...])
    if valid < q:
        dt = jnp.where(lax.broadcasted_iota(jnp.int32, dt.shape, 0) < valid, dt, 0.0)
    a = dt * (-jnp.exp(alog_ref[...]))
    acum = _exact_dot_left(tri.astype(BF16), a)
    acum_t = acum.T
    spread = spread_ref[...]
    dt_x = _spread_dot(dt, spread)
    grow_x = _spread_dot(jnp.exp(acum), spread)
    wst_x = _spread_dot(jnp.exp(acum[q - 1:q, :] - acum), spread)
    chunk_decay = jnp.exp(acum_t[:, q - 1:q])

    xd = xs * dt_x
    xd_b = xd.astype(BF16)
    xw_b = (xd * wst_x).astype(BF16)
    y_parts = []
    for g in range(SSM_GROUPS):
        ncols = slice(g * SSM_STATE, (g + 1) * SSM_STATE)
        cb = _dot_nt(cm[:, ncols], bm[:, ncols])
        heads = range(g * SSM_HPG, (g + 1) * SSM_HPG)
        s_g = jnp.concatenate([state_ref[h] for h in heads], axis=0)
        gcols = slice(g * SSM_HPG * SSM_HEAD_DIM, (g + 1) * SSM_HPG * SSM_HEAD_DIM)
        y_off = _dot_nt(cm[:, ncols], s_g.astype(BF16)) * grow_x[:, gcols]
        y_diag = []
        for h in heads:
            seg = acum[:, h:h + 1] - acum_t[h:h + 1, :]
            lmat = (cb * jnp.exp(jnp.where(tri, seg, -jnp.inf))).astype(BF16)
            y_diag.append(_dot(lmat, xd_b[:, h * SSM_HEAD_DIM:(h + 1) * SSM_HEAD_DIM]))
        y_parts.append(jnp.concatenate(y_diag, axis=1) + y_off)
        new_states = _dot_tn(xw_b[:, gcols], bm[:, ncols])
        for i, h in enumerate(heads):
            state_ref[h] = state_ref[h] * chunk_decay[h:h + 1, :] + new_states[i * SSM_HEAD_DIM:(i + 1) * SSM_HEAD_DIM, :]
    y = jnp.concatenate(y_parts, axis=1) + xs * dskip_ref[...]
    zf = z_ref[...]
    y = y * (zf * jax.nn.sigmoid(zf))
    gw = SSM_INNER // SSM_GROUPS
    out = []
    for g in range(SSM_GROUPS):
        yg = y[:, g * gw:(g + 1) * gw]
        out.append(_rms(yg, ng_ref[:, g * gw:(g + 1) * gw]))
    y_ref[...] = jnp.concatenate(out, axis=1).astype(BF16)

    if state_out_every_step:
        hf_ref[...] = state_ref[...]
    else:
        @pl.when(c == pl.num_programs(1) - 1)
        def _():
            hf_ref[...] = state_ref[...]


def _ssd_kernel(*refs, valid):
    _ssd_body(*refs, valid=valid)


def _ssd_decode_kernel(pt_ref, *refs, valid, pages, n_q, lam_init, steps_per_seq):
    ssd_in, rest = refs[:N_SSD_IN], refs[N_SSD_IN:]
    lam_ref, g_ref, q_ref, kn_ref, vn_ref = rest[:5]
    k_refs, v_refs = rest[5:5 + pages], rest[5 + pages:5 + 2 * pages]
    y_ref, hf_ref, o_ref, state_ref, tail_ref, m_ref, l_ref, acc_ref = rest[5 + 2 * pages:]
    t = pl.program_id(0) * pl.num_programs(1) + pl.program_id(1)
    pl.when(t == 0)(functools.partial(_zero_decode_stats, m_ref, l_ref, acc_ref))
    _ssd_body(*ssd_in, y_ref, hf_ref, state_ref, tail_ref, valid=valid, state_out_every_step=True)
    part = t % steps_per_seq
    _decode_step(lam_ref, g_ref, q_ref, kn_ref, vn_ref, k_refs, v_refs, o_ref, m_ref, l_ref, acc_ref,
                 first=part == 0, last=part == steps_per_seq - 1, n_q=n_q, lam_init=lam_init)


def ssd_scan(z, xbc, dt_raw, conv_buf, h0, conv_w, conv_b, dt_bias, a_log, d_skip, norm_g, *,
             batch, chunk_len, valid, decode=None):
    m = z.shape[0]
    seq = m // batch
    q = chunk_len
    assert seq % q == 0 and (valid == q or seq == q)
    nc = seq // q
    z3 = z.reshape(batch, seq, SSM_INNER)
    x3 = xbc.reshape(batch, seq, SSM_CONV_DIM)
    d3 = dt_raw.reshape(batch, seq, LANE)
    cbuf = jnp.pad(conv_buf, ((0, 0), (SUBLANE - (SSM_CONV - 1), 0), (0, 0)))
    spread = (jnp.arange(SSM_INNER)[None, :] // SSM_HEAD_DIM == jnp.arange(LANE)[:, None]).astype(BF16)
    lane_pad = lambda v: jnp.pad(v.reshape(1, -1), ((0, 0), (0, LANE - v.shape[-1])))
    chunk = lambda w: pl.BlockSpec((None, q, w), lambda b, c, *_: (b, c, 0))
    state_spec = pl.BlockSpec((None, SSM_HEADS, SSM_HEAD_DIM, SSM_STATE), lambda b, c, *_: (b, 0, 0, 0))
    const = lambda shape: pl.BlockSpec(shape, lambda b, c, *_: (0,) * len(shape))
    in_specs = [chunk(SSM_INNER), chunk(SSM_CONV_DIM), chunk(LANE),
                pl.BlockSpec((None, SUBLANE, SSM_CONV_DIM), lambda b, c, *_: (b, 0, 0)), state_spec,
                const((SSM_CONV, SSM_CONV_DIM)), const((1, SSM_CONV_DIM)), const((1, LANE)),
                const((1, LANE)), const((1, SSM_INNER)), const((1, SSM_INNER)),
                const((LANE, SSM_INNER))]
    assert len(in_specs) == N_SSD_IN
    operands = (z3, x3, d3, cbuf, h0, conv_w, conv_b.reshape(1, -1), lane_pad(dt_bias), lane_pad(a_log),
                jnp.repeat(d_skip, SSM_HEAD_DIM).reshape(1, -1), norm_g.reshape(1, -1), spread)
    out_specs = [chunk(SSM_INNER), state_spec]
    out_shape = [jax.ShapeDtypeStruct((batch, seq, SSM_INNER), BF16), jax.ShapeDtypeStruct(h0.shape, F32)]
    scratch = [pltpu.VMEM((SSM_HEADS, SSM_HEAD_DIM, SSM_STATE), F32), pltpu.VMEM((SUBLANE, SSM_CONV_DIM), F32)]
    if decode is None:
        y, hf = pl.pallas_call(
            functools.partial(_ssd_kernel, valid=valid),
            grid=(batch, nc), in_specs=in_specs, out_specs=out_specs, out_shape=out_shape,
            scratch_shapes=scratch, compiler_params=_cparams("parallel", "arbitrary"), name="ssd_scan",
        )(*operands)
        return y.reshape(m, SSM_INNER), hf

    b_dec, n_q, hw = decode["qb"].shape
    n_pages = decode["page_table"].shape[1]
    steps_per_seq, pages = _decode_split(batch * nc, b_dec, n_pages)
    pt, qrows, kn, vn, ck, cv = _decode_operands(decode["qb"], decode["kb_new"], decode["vb_new"],
                                                 decode["cache_k"], decode["cache_v"],
                                                 decode["page_table"], decode["layer"])
    n_rows = qrows.shape[1]
    step_of = lambda b, c: b * nc + c

    def page_spec(i):
        return pl.BlockSpec((None,) + ck.shape[1:], lambda b, c, pt_ref: (pt_ref[step_of(b, c) * pages + i], 0, 0))

    per_seq = lambda r, w: pl.BlockSpec((None, r, w), lambda b, c, pt_ref: (step_of(b, c) // steps_per_seq, 0, 0))
    dec_specs = [const((4, ATTN_DK)), const((1, ATTN_DV)), per_seq(n_rows, ATTN_HD), per_seq(NEW_ROWS, hw),
                 per_seq(NEW_ROWS, hw)] + [page_spec(i) for i in range(pages)] * 2
    grid_spec = pltpu.PrefetchScalarGridSpec(
        num_scalar_prefetch=1, grid=(batch, nc),
        in_specs=in_specs + dec_specs,
        out_specs=out_specs + [per_seq(n_q, hw)],
        scratch_shapes=scratch + _decode_scratch(n_q),
    )
    y, hf, o_dec = pl.pallas_call(
        functools.partial(_ssd_decode_kernel, valid=valid, pages=pages, n_q=n_q,
                          lam_init=decode["lam_init"], steps_per_seq=steps_per_seq),
        grid_spec=grid_spec,
        out_shape=out_shape + [jax.ShapeDtypeStruct((b_dec, n_q, hw), BF16)],
        compiler_params=_cparams("arbitrary", "arbitrary"),
        name="ssd_scan_decode",
    )(pt, *operands, decode["lam_p"], decode["subln_g"].reshape(1, ATTN_DV), qrows, kn, vn,
      *([ck] * pages), *([cv] * pages))
    return y.reshape(m, SSM_INNER), hf, o_dec


def _ssm_group(h, batch, seq, conv_buf, h0, g_mix, w_in_all, layer, w_dt, conv_w, conv_b, dt_bias, a_log,
               d_skip, norm_g, decode=None):
    d = h.shape[1]
    if seq % SSM_CHUNK == 0:
        padded = seq
        chunk_len = valid = SSM_CHUNK
        hin = h
    else:
        assert seq <= SSM_SHORT_CHUNK
        padded = chunk_len = SSM_SHORT_CHUNK
        valid = seq
        hin = jnp.pad(h.reshape(batch, seq, d), ((0, 0), (0, padded - seq), (0, 0))).reshape(-1, d)
    z, xbc, dt = ssm_in_proj(hin, g_mix, w_in_all, layer, w_dt)
    y, hf, *dec_out = ssd_scan(z, xbc, dt, conv_buf, h0, conv_w, conv_b, dt_bias, a_log, d_skip, norm_g,
                               batch=batch, chunk_len=chunk_len, valid=valid, decode=decode)
    xbc3 = xbc.reshape(batch, padded, SSM_CONV_DIM)[:, :seq]
    new_buf = jnp.concatenate([conv_buf, xbc3], axis=1)[:, seq:]
    y = y.reshape(batch, padded, SSM_INNER)[:, :seq].reshape(batch * seq, SSM_INNER)
    return (y, new_buf, hf, *dec_out)


def kernel(x_prompt, x_sample, cache_k, cache_v, state_ssm, state_conv, page_table, p_prompt, p_sample, norm_mix_g, norm_mlp_g, norm_ple_g, mlp_w_up, mlp_w_down, ple_w_gate, ple_w_proj, final_norm_g, sgu_w_in, sgu_ln_g, sgu_ln_b, sgu_w_s, sgu_b_s, sgu_w_out, attn_w_qkv, attn_lambda, attn_subln_g, attn_w_out, ssm_w_in, ssm_conv_w, ssm_conv_b, ssm_dt_bias, ssm_a_log, ssm_d, ssm_norm_g, ssm_w_out):
    b_p, l_p, d = x_prompt.shape
    b_s, l_s, _ = x_sample.shape
    depth = norm_mix_g.shape[0]
    past_len = page_table.shape[1] * cache_k.shape[2]
    m_s = b_s * l_s
    assert m_s % SGU_CHUNK == 0 and SGU_CHUNK % l_s == 0 and l_p % SGU_CHUNK == 0
    bf = lambda w: w.astype(BF16)

    hp = x_prompt.reshape(b_p * l_p, d)
    hs = x_sample.reshape(m_s, d)
    rope_p = _rope_tables(jnp.arange(l_p, dtype=jnp.int32))
    rope_s = _rope_tables(past_len + jnp.arange(l_s, dtype=jnp.int32))

    w_up, w_down, w_gate, w_proj = bf(mlp_w_up), bf(mlp_w_down), bf(ple_w_gate), bf(ple_w_proj)
    w_sgu_in, w_sgu_out = bf(sgu_w_in), bf(sgu_w_out)
    w_qkv, w_attn_out = bf(attn_w_qkv), bf(attn_w_out)
    w_ssm_in, w_ssm_out = bf(ssm_w_in), bf(ssm_w_out)
    pp = p_prompt.reshape(depth, b_p * l_p, D_PLE)
    ps = p_sample.reshape(depth, m_s, D_PLE)

    k_p, v_p, k_s, v_s = [], [], [], []
    ssm_p, conv_p, ssm_s, conv_s = [], [], [], []
    sgu_s = []

    def channel(h, mix, p_all, i, w_mix, j, tile):
        return channel_update(h, mix, p_all, i, w_mix, j, norm_mlp_g[i], w_up, w_down, norm_ple_g[i],
                              w_gate, w_proj, final_norm_g, final_norm=i == depth - 1, tile=tile)

    def decode_rides_along(i):
        if i + 1 >= depth or (i + 1) % N_MIXERS != 2 or l_p % SSM_CHUNK:
            return False
        return _decode_split(b_p * (l_p // SSM_CHUNK), b_s, page_table.shape[1]) is not None

    pending = None
    for i in range(depth):
        kind, j = i % N_MIXERS, i // N_MIXERS
        g_mix = norm_mix_g[i]
        if kind == 0:
            ws_p, bs_p = _sgu_spatial(sgu_w_s[j], sgu_b_s[j], l_p)
            ws_s, bs_s = _sgu_spatial(sgu_w_s[j], sgu_b_s[j], l_s)
            mix_p, _ = sgu_mix(hp, g_mix, w_sgu_in, j, sgu_ln_g[j], sgu_ln_b[j], ws_p, bs_p, want_v=False)
            mix_s, v_rows = sgu_mix(hs, g_mix, w_sgu_in, j, sgu_ln_g[j], sgu_ln_b[j], ws_s, bs_s,
                                    want_v=True, tile=SMALL_TILE)
            sgu_s.append(v_rows.reshape(b_s, l_s, SGU_WIDTH))
            w_mix = w_sgu_out
        elif kind == 1:
            lam_init = 0.8 - 0.6 * math.exp(-0.3 * i)
            qb, k, v, kb, vb = qkv_rope(hp, g_mix, w_qkv, j, *rope_p, transposed_v=True)
            mix_p = attn_prompt(qb, kb, vb, attn_lambda[j], attn_subln_g[j], batch=b_p, lam_init=lam_init)
            k_p.append(k.reshape(b_p, l_p, ATTN_HEADS, ATTN_HD))
            v_p.append(v.reshape(b_p, l_p, ATTN_HEADS, ATTN_DV))
            qb, k, v, kb, vb = qkv_rope(hs, g_mix, w_qkv, j, *rope_s, tile=SMALL_TILE)
            k_s.append(k.reshape(b_s, l_s, ATTN_HEADS, ATTN_HD))
            v_s.append(v.reshape(b_s, l_s, ATTN_HEADS, ATTN_DV))
            seq3 = lambda x: x.reshape(b_s, l_s, -1)
            dec = dict(qb=seq3(qb), kb_new=seq3(kb), vb_new=seq3(vb), cache_k=cache_k, cache_v=cache_v,
                       page_table=page_table, layer=j, lam_p=attn_lambda[j], subln_g=attn_subln_g[j],
                       lam_init=lam_init)
            w_mix = w_attn_out
            if decode_rides_along(i):
                pending = (dec, (ps, i, w_mix, j))
                mix_s = None
            else:
                mix_s = attn_decode(**dec).reshape(m_s, -1)
        else:
            w_dt = jnp.pad(w_ssm_in[j][:, SSM_INNER + SSM_CONV_DIM:], ((0, 0), (0, LANE - SSM_HEADS)))
            shared = (g_mix, w_ssm_in, j, w_dt, ssm_conv_w[j], ssm_conv_b[j], ssm_dt_bias[j],
                      ssm_a_log[j], ssm_d[j], ssm_norm_g[j])
            zero_buf = jnp.zeros((b_p, SSM_CONV - 1, SSM_CONV_DIM), F32)
            zero_h = jnp.zeros((b_p, SSM_HEADS, SSM_HEAD_DIM, SSM_STATE), F32)
            if pending is None:
                mix_p, cb_p, hf_p = _ssm_group(hp, b_p, l_p, zero_buf, zero_h, *shared)
            else:
                dec, prev_channel = pending
                pending = None
                mix_p, cb_p, hf_p, o_dec = _ssm_group(hp, b_p, l_p, zero_buf, zero_h, *shared, decode=dec)
                hs = channel(hs, o_dec.reshape(m_s, -1), *prev_channel, SMALL_TILE)
            mix_s, cb_s, hf_s = _ssm_group(hs, b_s, l_s, state_conv[j], state_ssm[j], *shared)
            ssm_p.append(hf_p)
            conv_p.append(cb_p)
            ssm_s.append(hf_s)
            conv_s.append(cb_s)
            w_mix = w_ssm_out
        hp = channel(hp, mix_p, pp, i, w_mix, j, ROW_TILE)
        if mix_s is not None:
            hs = channel(hs, mix_s, ps, i, w_mix, j, SMALL_TILE)

    return (hp.reshape(b_p, l_p, d), hs.reshape(b_s, l_s, d),
            jnp.stack(k_p), jnp.stack(v_p), jnp.stack(k_s), jnp.stack(v_s),
            jnp.stack(ssm_p), jnp.stack(conv_p), jnp.stack(ssm_s), jnp.stack(conv_s),
            jnp.stack(sgu_s))
```

```python
import functools
import math

import jax
import jax.numpy as jnp
from jax import lax
from jax.experimental import pallas as pl
from jax.experimental.pallas import tpu as pltpu

F32 = jnp.float32
BF16 = jnp.bfloat16

EPS = 1e-6
D_MODEL = 1024
D_FF = 4 * D_MODEL
D_PLE = 256
N_MIXERS = 3

SGU_CHUNK = 128
SGU_WIDTH = 2 * D_MODEL
SGU_GROUPS = 8
SGU_GDIM = SGU_WIDTH // SGU_GROUPS

ATTN_HEADS = 8
ATTN_DK = 64
ATTN_DV = 128
ATTN_HD = 2 * ATTN_DK
ROPE_THETA = 10000.0

SSM_INNER = 2 * D_MODEL
SSM_HEAD_DIM = 64
SSM_HEADS = SSM_INNER // SSM_HEAD_DIM
SSM_GROUPS = 4
SSM_STATE = 128
SSM_CONV = 4
SSM_CHUNK = 128
SSD_SEQ_GROUP = 4
SSM_SHORT_CHUNK = 16
SSM_GN = SSM_GROUPS * SSM_STATE
SSM_CONV_DIM = SSM_INNER + 2 * SSM_GN
SSM_HPG = SSM_HEADS // SSM_GROUPS

VMEM_LIMIT_BYTES = 56 * 1024 * 1024
LANE = 128
SUBLANE = 8

ROW_TILE = 512
SGU_ROW_TILE = 256
ATTN_TILE = 1024
SMALL_TILE = SGU_CHUNK


def _cparams(*sem):
    return pltpu.CompilerParams(dimension_semantics=sem, vmem_limit_bytes=VMEM_LIMIT_BYTES)


def _const_spec(shape):
    zeros = (0,) * len(shape)
    return pl.BlockSpec(shape, lambda *_: zeros, pipeline_mode=pl.Buffered(1))


def _layer_spec(stack, layer):
    tail = tuple(stack.shape[1:])
    zeros = (0,) * len(tail)
    return pl.BlockSpec((None,) + tail, lambda *_: (layer,) + zeros, pipeline_mode=pl.Buffered(1))


def _rms(x, g):
    ms = jnp.mean(x * x, axis=-1, keepdims=True)
    return (x * lax.rsqrt(ms + EPS)) * g


def _dot(a, b):
    return jnp.dot(a, b, preferred_element_type=F32)


def _dot_nt(a, b):
    return lax.dot_general(a, b, (((1,), (1,)), ((), ())), preferred_element_type=F32)


def _dot_tn(a, b):
    return lax.dot_general(a, b, (((0,), (0,)), ((), ())), preferred_element_type=F32)


def _row_tile(m, want):
    t = min(m, want)
    assert m % t == 0, (m, t)
    return t


def _channel_kernel(h_ref, mix_ref, p_ref, wmix_ref, gmlp_ref, wup_ref, wdown_ref, gple_ref,
                    wgate_ref, wproj_ref, gfin_ref, out_ref, *, final_norm):
    h = h_ref[...] + _dot(mix_ref[...], wmix_ref[...])
    xn = _rms(h, gmlp_ref[...]).astype(BF16)
    a = jnp.square(jnp.maximum(_dot(xn, wup_ref[...]), 0.0)).astype(BF16)
    h = h + _dot(a, wdown_ref[...])
    xn = _rms(h, gple_ref[...]).astype(BF16)
    gate = jax.nn.sigmoid(_dot(xn, wgate_ref[...]))
    h = h + gate * _dot(p_ref[...].astype(BF16), wproj_ref[...])
    if final_norm:
        h = _rms(h, gfin_ref[...])
    out_ref[...] = h


def channel_update(h, mix, p_all, layer, w_mix_all, mix_layer, g_mlp, w_up_all, w_down_all, g_ple,
                   w_gate_all, w_proj_all, g_fin, *, final_norm, tile=ROW_TILE):
    m, d = h.shape
    dm = mix.shape[1]
    tm = _row_tile(m, tile)
    row = lambda w: pl.BlockSpec((tm, w), lambda i: (i, 0))
    return pl.pallas_call(
        functools.partial(_channel_kernel, final_norm=final_norm),
        grid=(m // tm,),
        in_specs=[row(d), row(dm), pl.BlockSpec((None, tm, D_PLE), lambda i: (layer, i, 0)),
                  _layer_spec(w_mix_all, mix_layer), _const_spec((1, d)),
                  _layer_spec(w_up_all, layer), _layer_spec(w_down_all, layer), _const_spec((1, d)),
                  _layer_spec(w_gate_all, layer), _layer_spec(w_proj_all, layer), _const_spec((1, d))],
        out_specs=row(d),
        out_shape=jax.ShapeDtypeStruct((m, d), F32),
        compiler_params=_cparams("parallel"),
        name="channel_update",
    )(h, mix, p_all, w_mix_all, g_mlp.reshape(1, d), w_up_all, w_down_all, g_ple.reshape(1, d),
      w_gate_all, w_proj_all, g_fin.reshape(1, d))


def _sgu_kernel(h_ref, g_ref, win_ref, lng_ref, lnb_ref, ws_ref, bs_ref, mix_ref, *v_out,
                n_chunks):
    xn = _rms(h_ref[...], g_ref[...]).astype(BF16)
    v = jax.nn.gelu(_dot(xn, win_ref[:, SGU_WIDTH:]))
    u = jax.nn.gelu(_dot(xn, win_ref[:, :SGU_WIDTH]))
    mu = jnp.mean(v, axis=-1, keepdims=True)
    vc = v - mu
    v = (vc * lax.rsqrt(jnp.mean(vc * vc, axis=-1, keepdims=True) + EPS)) * lng_ref[...] + lnb_ref[...]
    if v_out:
        v_out[0][...] = v
    vb = v.astype(BF16)
    bs = bs_ref[...]
    for c in range(n_chunks):
        rows = slice(c * SGU_CHUNK, (c + 1) * SGU_CHUNK)
        for g in range(SGU_GROUPS):
            cols = slice(g * SGU_GDIM, (g + 1) * SGU_GDIM)
            s = _dot(ws_ref[g], vb[rows, cols]) + bs[:, g:g + 1]
            mix_ref[rows, cols] = (u[rows, cols] * s).astype(BF16)


def _sgu_spatial(w_s, b_s, seq):
    q = min(SGU_CHUNK, seq)
    tri = jnp.tril(jnp.ones((q, q), dtype=bool))
    ws = jnp.where(tri, w_s[:, :q, :q], 0)
    bs = b_s[:, :q]
    rep = SGU_CHUNK // q
    if rep > 1:
        eye = jnp.eye(rep, dtype=ws.dtype)
        ws = jnp.einsum("ab,gts->gatbs", eye, ws).reshape(SGU_GROUPS, SGU_CHUNK, SGU_CHUNK)
        bs = jnp.tile(bs, (1, rep))
    return ws.astype(BF16), bs.T


def sgu_mix(h, g_mix, w_in_all, layer, ln_g, ln_b, ws, bs, *, want_v, tile=SGU_ROW_TILE):
    m, d = h.shape
    tm = _row_tile(m, tile)
    row = lambda w: pl.BlockSpec((tm, w), lambda i: (i, 0))
    out_shape = [jax.ShapeDtypeStruct((m, SGU_WIDTH), BF16)]
    out_specs = [row(SGU_WIDTH)]
    if want_v:
        out_shape.append(jax.ShapeDtypeStruct((m, SGU_WIDTH), F32))
        out_specs.append(row(SGU_WIDTH))
    outs = pl.pallas_call(
        functools.partial(_sgu_kernel, n_chunks=tm // SGU_CHUNK),
        grid=(m // tm,),
        in_specs=[row(d), _const_spec((1, d)), _layer_spec(w_in_all, layer),
                  _const_spec((1, SGU_WIDTH)), _const_spec((1, SGU_WIDTH)),
                  _const_spec((SGU_GROUPS, SGU_CHUNK, SGU_CHUNK)),
                  _const_spec((SGU_CHUNK, SGU_GROUPS))],
        out_specs=out_specs,
        out_shape=out_shape,
        compiler_params=_cparams("parallel"),
        name="sgu_mix",
    )(h, g_mix.reshape(1, d), w_in_all, ln_g.reshape(1, -1), ln_b.reshape(1, -1), ws, bs)
    return outs if want_v else (outs[0], None)


def _rope_tables(pos):
    half = ATTN_DK // 2
    inv = ROPE_THETA ** (-jnp.arange(half, dtype=F32) / half)
    ang = pos.astype(F32)[:, None] * inv[None, :]
    cos, sin = jnp.cos(ang), jnp.sin(ang)
    cos_t = jnp.tile(cos, (1, 4))
    sin_t = jnp.tile(jnp.concatenate([-sin, sin], axis=1), (1, 2))
    return cos_t, sin_t


def _rope_head(x, cos, sin_signed, first_half):
    partner = jnp.where(first_half, pltpu.roll(x, LANE - ATTN_DK // 2, 1),
                        pltpu.roll(x, ATTN_DK // 2, 1))
    return x * cos + partner * sin_signed


def _qkv_kernel(h_ref, g_ref, w_ref, cos_ref, sin_ref, qb_ref, k_ref, v_ref, kb_ref, vb_ref, *,
                transposed_v):
    xn = _rms(h_ref[...], g_ref[...]).astype(BF16)
    qkv = _dot(xn, w_ref[...])
    cos = cos_ref[...]
    sin = sin_ref[...]
    lane = lax.broadcasted_iota(jnp.int32, cos.shape, 1)
    first_half = (lane % ATTN_DK) < (ATTN_DK // 2)
    qk_dim = ATTN_HEADS * ATTN_HD
    scale = ATTN_DK ** -0.5 * math.log2(math.e)
    rows = h_ref.shape[0]
    for hd in range(ATTN_HEADS):
        cols = slice(hd * ATTN_HD, (hd + 1) * ATTN_HD)
        q = _rope_head(qkv[:, cols], cos, sin, first_half)
        k = _rope_head(qkv[:, qk_dim + hd * ATTN_HD:qk_dim + (hd + 1) * ATTN_HD], cos, sin,
                       first_half)
        qb_ref[:, cols] = (q * scale).astype(BF16)
        k_ref[pl.ds(hd, rows, stride=ATTN_HEADS), :] = k
        kb_ref[:, cols] = k.astype(BF16)
        v_ref[pl.ds(hd, rows, stride=ATTN_HEADS), :] = qkv[:, 2 * qk_dim + hd * ATTN_DV:
                                                           2 * qk_dim + (hd + 1) * ATTN_DV]
    v = qkv[:, 2 * qk_dim:]
    vb_ref[...] = (v.T if transposed_v else v).astype(BF16)


def qkv_rope(h, g_mix, w_qkv_all, layer, cos_t, sin_t, *, transposed_v=False, tile=ROW_TILE):
    m, d = h.shape
    seq = cos_t.shape[0]
    tm = _row_tile(m, tile)
    row = lambda w: pl.BlockSpec((tm, w), lambda i: (i, 0))
    if seq % tm == 0:
        n_blk = seq // tm
        table = pl.BlockSpec((tm, LANE), lambda i: (i % n_blk, 0))
    else:
        assert tm % seq == 0, (tm, seq)
        cos_t = jnp.tile(cos_t, (tm // seq, 1))
        sin_t = jnp.tile(sin_t, (tm // seq, 1))
        table = _const_spec((tm, LANE))
    hw = ATTN_HEADS * ATTN_HD
    vb_spec, vb_shape = row(hw), jax.ShapeDtypeStruct((m, hw), BF16)
    head_rows = pl.BlockSpec((tm * ATTN_HEADS, ATTN_HD), lambda i: (i, 0))
    if transposed_v:
        assert seq % tm == 0
        n_blk = seq // tm
        vb_spec = pl.BlockSpec((None, hw, tm), lambda i: (i // n_blk, 0, i % n_blk))
        vb_shape = jax.ShapeDtypeStruct((m // seq, hw, seq), BF16)
    return pl.pallas_call(
        functools.partial(_qkv_kernel, transposed_v=transposed_v),
        grid=(m // tm,),
        in_specs=[row(d), _const_spec((1, d)), _layer_spec(w_qkv_all, layer), table, table],
        out_specs=[row(hw), head_rows, head_rows, row(hw), vb_spec],
        out_shape=[jax.ShapeDtypeStruct((m, hw), BF16),
                   jax.ShapeDtypeStruct((m * ATTN_HEADS, ATTN_HD), F32),
                   jax.ShapeDtypeStruct((m * ATTN_HEADS, ATTN_DV), F32),
                   jax.ShapeDtypeStruct((m, hw), BF16), vb_shape],
        compiler_params=_cparams("parallel"),
        name="qkv_rope",
    )(h, g_mix.reshape(1, d), w_qkv_all, cos_t, sin_t)


def _lambda_value(lam_ref, lam_init):
    lp = lam_ref[...]
    s1 = jnp.sum(lp[0:1] * lp[1:2], axis=-1, keepdims=True)
    s2 = jnp.sum(lp[2:3] * lp[3:4], axis=-1, keepdims=True)
    return jnp.exp(s1) - jnp.exp(s2) + lam_init


def _softmax_step_t(s, v, m_prev, acc_prev):
    m_new = jnp.maximum(m_prev, jnp.max(s, axis=0, keepdims=True))
    alpha = jnp.exp2(m_prev - m_new)
    p = jnp.exp2(s - m_new)
    acc = alpha * acc_prev + _dot(v, p.astype(BF16))
    return m_new, acc


def _attn_prompt_kernel(lam_ref, g_ref, q_ref, k_ref, v_ref, o_ref, *, tile, n_tiles, lam_init):
    qi = pl.program_id(2)
    half = tile // 2
    q = q_ref[...]
    lane = lax.broadcasted_iota(jnp.int32, q.shape, 1)
    zero = jnp.zeros_like(q)
    q1 = jnp.where(lane < ATTN_DK, q, zero)
    q2 = jnp.where(lane < ATTN_DK, zero, q)

    def block(start, n_keys, q_lo, carry, masked):
        k = k_ref[pl.ds(start, n_keys), :]
        ones_row = (lax.broadcasted_iota(jnp.int32, (ACC_PAD, n_keys), 0) == 0).astype(BF16)
        v = jnp.concatenate([v_ref[:, pl.ds(start, n_keys)], ones_row], axis=0)
        s1 = _dot_nt(k, q1[q_lo:])
        s2 = _dot_nt(k, q2[q_lo:])
        if masked:
            kr = lax.broadcasted_iota(jnp.int32, s1.shape, 0)
            qc = lax.broadcasted_iota(jnp.int32, s1.shape, 1)
            s1 = jnp.where(kr <= qc, s1, -jnp.inf)
            s2 = jnp.where(kr <= qc, s2, -jnp.inf)
        c1, c2 = carry
        return _softmax_step_t(s1, v, *c1), _softmax_step_t(s2, v, *c2)

    init = (jnp.full((1, tile), -jnp.inf, F32), jnp.zeros((ATTN_DV + ACC_PAD, tile), F32))
    lam = _lambda_value(lam_ref, lam_init)

    def tile_at(n_full):
        carry = (init, init)
        for ki in range(n_full):
            carry = block(ki * tile, tile, 0, carry, False)
        base = n_full * tile
        carry = block(base, half, 0, carry, True)
        tail = tuple(tuple(x[:, half:] for x in c) for c in carry)
        tail = block(base + half, half, half, tail, True)
        (_, a1), (_, a2) = tuple(
            tuple(jnp.concatenate([x[:, :half], y], axis=1) for x, y in zip(c, t)) for c, t in zip(carry, tail))
        normed = lambda a: a[:ATTN_DV] / a[ATTN_DV:ATTN_DV + 1]
        o = (normed(a1) - lam * normed(a2)).T
        o = _rms(o, g_ref[...]) * (1.0 - lam_init)
        o_ref[...] = o.astype(BF16)

    for n_full in range(n_tiles):
        pl.when(qi == n_full)(functools.partial(tile_at, n_full))


def attn_prompt(qb, kb, vb_t, lam_p, subln_g, *, batch, lam_init, tile=ATTN_TILE):
    m, hw = qb.shape
    seq = m // batch
    t = _row_tile(seq, tile)
    q3, k3 = (x.reshape(batch, seq, hw) for x in (qb, kb))
    q_spec = pl.BlockSpec((None, t, ATTN_HD), lambda b, h, qi: (b, qi, h))
    k_spec = pl.BlockSpec((None, seq, ATTN_HD), lambda b, h, qi: (b, 0, h))
    vt_spec = pl.BlockSpec((None, ATTN_DV, seq), lambda b, h, qi: (b, h, 0))
    const = lambda shape: pl.BlockSpec(shape, lambda b, h, qi: (0,) * len(shape))
    out = pl.pallas_call(
        functools.partial(_attn_prompt_kernel, tile=t, n_tiles=seq // t, lam_init=lam_init),
        grid=(batch, ATTN_HEADS, seq // t),
        in_specs=[const((4, ATTN_DK)), const((1, ATTN_DV)), q_spec, k_spec, vt_spec],
        out_specs=q_spec,
        out_shape=jax.ShapeDtypeStruct((batch, seq, hw), BF16),
        compiler_params=_cparams("parallel", "parallel", "parallel"),
        name="attn_prompt",
    )(lam_p, subln_g.reshape(1, ATTN_DV), q3, k3, vb_t)
    return out.reshape(m, hw)


NEW_ROWS = 16
ACC_PAD = 16


def _decode_step(lam_ref, g_ref, q_ref, kn_ref, vn_ref, k_refs, v_refs, o_ref, m_ref, l_ref, acc_ref, *,
                 first, last, n_q, lam_init):
    rows = 2 * n_q
    q_all = q_ref[...].astype(BF16)

    def update(s, pv_fn, fresh):
        m_prev = jnp.where(fresh, -jnp.inf, m_ref[...])
        m_new = jnp.maximum(m_prev, jnp.max(s, axis=1, keepdims=True))
        alpha = jnp.exp2(m_prev - m_new)
        p = jnp.exp2(s - m_new)
        l_ref[...] = jnp.where(fresh, 0.0, alpha * l_ref[...]) + jnp.sum(p, axis=1, keepdims=True)
        acc_ref[...] = jnp.where(fresh, 0.0, alpha * acc_ref[...]) + pv_fn(p)
        m_ref[...] = m_new

    n_cols = k_refs[0].shape[0]
    r = lax.broadcasted_iota(jnp.int32, (q_all.shape[0], n_cols), 0)
    c = lax.broadcasted_iota(jnp.int32, (q_all.shape[0], n_cols), 1)
    own_head = (c % ATTN_HEADS) == (r // rows)
    s = jnp.concatenate(
        [jnp.where(own_head, _dot_nt(q_all, k_ref[...].astype(BF16)), -jnp.inf) for k_ref in k_refs],
        axis=1)

    def pv_pages(p):
        acc = None
        for i, v_ref in enumerate(v_refs):
            d = _dot(p[:, i * n_cols:(i + 1) * n_cols].astype(BF16), v_ref[...].astype(BF16))
            acc = d if acc is None else acc + d
        return acc

    update(s, pv_pages, first)

    @pl.when(last)
    def _():
        kn = kn_ref[...]
        vn = vn_ref[...]
        head_cols = lambda x, h: x[:, h * ATTN_HD:(h + 1) * ATTN_HD]
        q_heads = [q_all[h * rows:(h + 1) * rows] for h in range(ATTN_HEADS)]
        s_new = jnp.concatenate([_dot_nt(q_heads[h], head_cols(kn, h)) for h in range(ATTN_HEADS)], axis=0)
        rn = lax.broadcasted_iota(jnp.int32, s_new.shape, 0)
        cn = lax.broadcasted_iota(jnp.int32, s_new.shape, 1)
        s_new = jnp.where(cn <= rn % n_q, s_new, -jnp.inf)

        def pv_new(p):
            return jnp.concatenate(
                [_dot(p[h * rows:(h + 1) * rows].astype(BF16), head_cols(vn, h)) for h in range(ATTN_HEADS)],
                axis=0)

        update(s_new, pv_new, False)
        lam = _lambda_value(lam_ref, lam_init)
        o_all = acc_ref[...] / l_ref[...]
        for h in range(ATTN_HEADS):
            r1 = h * rows
            o = o_all[r1:r1 + n_q] - lam * o_all[r1 + n_q:r1 + rows]
            o = _rms(o, g_ref[...]) * (1.0 - lam_init)
            o_ref[:, h * ATTN_DV:(h + 1) * ATTN_DV] = o.astype(BF16)


def _zero_decode_stats(m_ref, l_ref, acc_ref):
    m_ref[...] = jnp.zeros(m_ref.shape, F32)
    l_ref[...] = jnp.zeros(l_ref.shape, F32)
    acc_ref[...] = jnp.zeros(acc_ref.shape, F32)


def _attn_decode_kernel(pt_ref, lam_ref, g_ref, q_ref, kn_ref, vn_ref, *refs, pages, n_q, lam_init):
    o_ref, m_ref, l_ref, acc_ref = refs[2 * pages:]
    step = pl.program_id(1)
    pl.when(step == 0)(functools.partial(_zero_decode_stats, m_ref, l_ref, acc_ref))
    _decode_step(lam_ref, g_ref, q_ref, kn_ref, vn_ref, refs[:pages], refs[pages:2 * pages], o_ref,
                 m_ref, l_ref, acc_ref, first=step == 0, last=step == pl.num_programs(1) - 1,
                 n_q=n_q, lam_init=lam_init)


def _decode_operands(qb, kb_new, vb_new, cache_k, cache_v, page_table, layer):
    b, n_q, hw = qb.shape
    n_layers, n_pool, page, n_heads, hd = cache_k.shape
    assert n_q <= NEW_ROWS and (n_heads, hd) == (ATTN_HEADS, ATTN_HD)
    ck = cache_k.reshape(n_layers * n_pool, page * n_heads, hd)
    cv = cache_v.reshape(n_layers * n_pool, page * n_heads, hd)
    pt = (page_table.astype(jnp.int32) + layer * n_pool).reshape(-1)
    half = (jnp.arange(ATTN_HD) // ATTN_DK)[None, :] == jnp.arange(2)[:, None]
    q4 = qb.reshape(b, n_q, ATTN_HEADS, ATTN_HD).astype(F32)
    qrows = jnp.where(half[None, None, :, None, :], jnp.transpose(q4, (0, 2, 1, 3))[:, :, None], 0.0)
    qrows = qrows.reshape(b, ATTN_HEADS * 2 * n_q, ATTN_HD)
    pad = ((0, 0), (0, NEW_ROWS - n_q), (0, 0))
    return pt, qrows, jnp.pad(kb_new, pad), jnp.pad(vb_new, pad), ck, cv


def _decode_scratch(n_q):
    n_rows = ATTN_HEADS * 2 * n_q
    return [pltpu.VMEM((n_rows, 1), F32), pltpu.VMEM((n_rows, 1), F32), pltpu.VMEM((n_rows, ATTN_DV), F32)]


def _decode_split(steps, n_seq, n_pages):
    if steps % n_seq:
        return None
    steps_per_seq = steps // n_seq
    if n_pages % steps_per_seq:
        return None
    return steps_per_seq, n_pages // steps_per_seq


def attn_decode(qb, kb_new, vb_new, cache_k, cache_v, page_table, layer, lam_p, subln_g, *,
                lam_init, pages=8):
    b, n_q, hw = qb.shape
    n_pages = page_table.shape[1]
    assert n_pages % pages == 0
    pt, qrows, kn, vn, ck, cv = _decode_operands(qb, kb_new, vb_new, cache_k, cache_v, page_table, layer)
    n_rows = qrows.shape[1]

    def page_spec(i):
        return pl.BlockSpec((None,) + ck.shape[1:],
                            lambda bi, s, pt_ref: (pt_ref[bi * n_pages + s * pages + i], 0, 0))

    per_b = lambda r, w: pl.BlockSpec((None, r, w), lambda bi, s, pt_ref: (bi, 0, 0))
    const = lambda shape: pl.BlockSpec(shape, lambda bi, s, pt_ref: (0,) * len(shape))
    grid_spec = pltpu.PrefetchScalarGridSpec(
        num_scalar_prefetch=1,
        grid=(b, n_pages // pages),
        in_specs=[const((4, ATTN_DK)), const((1, ATTN_DV)), per_b(n_rows, ATTN_HD), per_b(NEW_ROWS, hw),
                  per_b(NEW_ROWS, hw)] + [page_spec(i) for i in range(pages)] * 2,
        out_specs=per_b(n_q, hw),
        scratch_shapes=_decode_scratch(n_q),
    )
    return pl.pallas_call(
        functools.partial(_attn_decode_kernel, pages=pages, n_q=n_q, lam_init=lam_init),
        grid_spec=grid_spec,
        out_shape=jax.ShapeDtypeStruct((b, n_q, hw), BF16),
        compiler_params=_cparams("parallel", "arbitrary"),
        name="attn_decode",
    )(pt, lam_p, subln_g.reshape(1, ATTN_DV), qrows, kn, vn, *([ck] * pages), *([cv] * pages))


def _ssm_in_kernel(h_ref, g_ref, w_ref, wdt_ref, z_ref, xbc_ref, dt_ref):
    xn = _rms(h_ref[...], g_ref[...]).astype(BF16)
    z_ref[...] = _dot(xn, w_ref[:, :SSM_INNER])
    xbc_ref[...] = _dot(xn, w_ref[:, SSM_INNER:SSM_INNER + SSM_CONV_DIM])
    dt_ref[...] = _dot(xn, wdt_ref[...])


def ssm_in_proj(h, g_mix, w_in_all, layer, w_dt, *, tile=ROW_TILE):
    m, d = h.shape
    tm = _row_tile(m, tile)
    row = lambda w: pl.BlockSpec((tm, w), lambda i: (i, 0))
    return pl.pallas_call(
        _ssm_in_kernel,
        grid=(m // tm,),
        in_specs=[row(d), _const_spec((1, d)), _layer_spec(w_in_all, layer), _const_spec((d, LANE))],
        out_specs=[row(SSM_INNER), row(SSM_CONV_DIM), row(LANE)],
        out_shape=[jax.ShapeDtypeStruct((m, SSM_INNER), F32),
                   jax.ShapeDtypeStruct((m, SSM_CONV_DIM), F32),
                   jax.ShapeDtypeStruct((m, LANE), F32)],
        compiler_params=_cparams("parallel"),
        name="ssm_in_proj",
    )(h, g_mix.reshape(1, d), w_in_all, w_dt)


def _split3(x):
    x1 = x.astype(BF16)
    r = x - x1.astype(F32)
    x2 = r.astype(BF16)
    x3 = (r - x2.astype(F32)).astype(BF16)
    return x1, x2, x3


def _spread_dot(x, sel):
    x1 = x.astype(BF16)
    x2 = (x - x1.astype(F32)).astype(BF16)
    return _dot(x1, sel) + _dot(x2, sel)


def _exact_dot_left(sel, x):
    x1, x2, x3 = _split3(x)
    return _dot(sel, x1) + _dot(sel, x2) + _dot(sel, x3)


N_SSD_IN = 12


def _ssd_body(z_ref, xbc_ref, dt_ref, cbuf_ref, h0_ref, cw_ref, cb_ref, dtb_ref, alog_ref,
              dskip_ref, ng_ref, spread_ref, y_ref, hf_ref, state_ref, tail_ref, *, valid,
              state_out_every_step=False):
    c = pl.program_id(1)
    q = xbc_ref.shape[0]

    @pl.when(c == 0)
    def _():
        state_ref[...] = h0_ref[...]
        tail_ref[...] = cbuf_ref[...]

    cur = xbc_ref[...]
    prev = tail_ref[...]
    sub = lax.broadcasted_iota(jnp.int32, prev.shape, 0)
    conv = cb_ref[...] + cur * cw_ref[SSM_CONV - 1:SSM_CONV, :]
    for tap in range(SSM_CONV - 1):
        shift = SSM_CONV - 1 - tap
        rolled = pltpu.roll(cur, shift, 0)
        head = jnp.where(sub < shift, pltpu.roll(prev, shift, 0), rolled[0:SUBLANE])
        shifted = jnp.concatenate([head, rolled[SUBLANE:]], axis=0)
        conv = conv + shifted * cw_ref[tap:tap + 1, :]
    tail_ref[...] = cur[q - SUBLANE:q]
    conv = conv * jax.nn.sigmoid(conv)
    xs = conv[:, :SSM_INNER]
    bm = conv[:, SSM_INNER:SSM_INNER + SSM_GN].astype(BF16)
    cm = conv[:, SSM_INNER + SSM_GN:].astype(BF16)

    row = lax.broadcasted_iota(jnp.int32, (q, q), 0)
    col = lax.broadcasted_iota(jnp.int32, (q, q), 1)
    tri = col <= row
    dt = jax.nn.softplus(dt_ref[...] + dtb_ref[...])
    if valid < q:
        dt = jnp.where(lax.broadcasted_iota(jnp.int32, dt.shape, 0) < valid, dt, 0.0)
    a = dt * (-jnp.exp(alog_ref[...]))
    acum = _exact_dot_left(tri.astype(BF16), a)
    acum_t = acum.T
    spread = spread_ref[...]
    dt_x = _spread_dot(dt, spread)
    grow_x = _spread_dot(jnp.exp(acum), spread)
    wst_x = _spread_dot(jnp.exp(acum[q - 1:q, :] - acum), spread)
    chunk_decay = jnp.exp(acum_t[:, q - 1:q])

    xd = xs * dt_x
    xd_b = xd.astype(BF16)
    xw_b = (xd * wst_x).astype(BF16)
    y_parts = []
    for g in range(SSM_GROUPS):
        ncols = slice(g * SSM_STATE, (g + 1) * SSM_STATE)
        cb = _dot_nt(cm[:, ncols], bm[:, ncols])
        heads = range(g * SSM_HPG, (g + 1) * SSM_HPG)
        s_g = jnp.concatenate([state_ref[h] for h in heads], axis=0)
        gcols = slice(g * SSM_HPG * SSM_HEAD_DIM, (g + 1) * SSM_HPG * SSM_HEAD_DIM)
        y_off = _dot_nt(cm[:, ncols], s_g.astype(BF16)) * grow_x[:, gcols]
        y_diag = []
        for h in heads:
            seg = acum[:, h:h + 1] - acum_t[h:h + 1, :]
            lmat = (cb * jnp.exp(jnp.where(tri, seg, -jnp.inf))).astype(BF16)
            y_diag.append(_dot(lmat, xd_b[:, h * SSM_HEAD_DIM:(h + 1) * SSM_HEAD_DIM]))
        y_parts.append(jnp.concatenate(y_diag, axis=1) + y_off)
        new_states = _dot_tn(xw_b[:, gcols], bm[:, ncols])
        for i, h in enumerate(heads):
            state_ref[h] = state_ref[h] * chunk_decay[h:h + 1, :] + new_states[i * SSM_HEAD_DIM:(i + 1) * SSM_HEAD_DIM, :]
    y = jnp.concatenate(y_parts, axis=1) + xs * dskip_ref[...]
    zf = z_ref[...]
    y = y * (zf * jax.nn.sigmoid(zf))
    gw = SSM_INNER // SSM_GROUPS
    out = []
    for g in range(SSM_GROUPS):
        yg = y[:, g * gw:(g + 1) * gw]
        out.append(_rms(yg, ng_ref[:, g * gw:(g + 1) * gw]))
    y_ref[...] = jnp.concatenate(out, axis=1).astype(BF16)

    if state_out_every_step:
        hf_ref[...] = state_ref[...]
    else:
        @pl.when(c == pl.num_programs(1) - 1)
        def _():
            hf_ref[...] = state_ref[...]


def _ssd_kernel(*refs, valid):
    _ssd_body(*refs, valid=valid)


SSD_SEQ_IDX = (0, 1, 2, 3, 4)


def _ssd_group_kernel(*refs, valid, group):
    ins, (y_ref, hf_ref, state_ref, tail_ref) = refs[:N_SSD_IN], refs[N_SSD_IN:]
    for g in range(group):
        per_seq = [r.at[g] if i in SSD_SEQ_IDX else r for i, r in enumerate(ins)]
        _ssd_body(*per_seq, y_ref.at[g], hf_ref.at[g], state_ref.at[g], tail_ref.at[g], valid=valid,
                  state_out_every_step=True)


def _ssd_decode_kernel(pt_ref, *refs, valid, pages, n_q, lam_init, steps_per_seq):
    ssd_in, rest = refs[:N_SSD_IN], refs[N_SSD_IN:]
    lam_ref, g_ref, q_ref, kn_ref, vn_ref = rest[:5]
    k_refs, v_refs = rest[5:5 + pages], rest[5 + pages:5 + 2 * pages]
    y_ref, hf_ref, o_ref, state_ref, tail_ref, m_ref, l_ref, acc_ref = rest[5 + 2 * pages:]
    t = pl.program_id(0) * pl.num_programs(1) + pl.program_id(1)
    pl.when(t == 0)(functools.partial(_zero_decode_stats, m_ref, l_ref, acc_ref))
    _ssd_body(*ssd_in, y_ref, hf_ref, state_ref, tail_ref, valid=valid, state_out_every_step=True)
    part = t % steps_per_seq
    _decode_step(lam_ref, g_ref, q_ref, kn_ref, vn_ref, k_refs, v_refs, o_ref, m_ref, l_ref, acc_ref,
                 first=part == 0, last=part == steps_per_seq - 1, n_q=n_q, lam_init=lam_init)


def ssd_scan(z, xbc, dt_raw, conv_buf, h0, conv_w, conv_b, dt_bias, a_log, d_skip, norm_g, *,
             batch, chunk_len, valid, decode=None):
    m = z.shape[0]
    seq = m // batch
    q = chunk_len
    assert seq % q == 0 and (valid == q or seq == q)
    nc = seq // q
    z3 = z.reshape(batch, seq, SSM_INNER)
    x3 = xbc.reshape(batch, seq, SSM_CONV_DIM)
    d3 = dt_raw.reshape(batch, seq, LANE)
    cbuf = jnp.pad(conv_buf, ((0, 0), (SUBLANE - (SSM_CONV - 1), 0), (0, 0)))
    spread = (jnp.arange(SSM_INNER)[None, :] // SSM_HEAD_DIM == jnp.arange(LANE)[:, None]).astype(BF16)
    lane_pad = lambda v: jnp.pad(v.reshape(1, -1), ((0, 0), (0, LANE - v.shape[-1])))
    chunk = lambda w: pl.BlockSpec((None, q, w), lambda b, c, *_: (b, c, 0))
    state_spec = pl.BlockSpec((None, SSM_HEADS, SSM_HEAD_DIM, SSM_STATE), lambda b, c, *_: (b, 0, 0, 0))
    const = lambda shape: pl.BlockSpec(shape, lambda b, c, *_: (0,) * len(shape))
    in_specs = [chunk(SSM_INNER), chunk(SSM_CONV_DIM), chunk(LANE),
                pl.BlockSpec((None, SUBLANE, SSM_CONV_DIM), lambda b, c, *_: (b, 0, 0)), state_spec,
                const((SSM_CONV, SSM_CONV_DIM)), const((1, SSM_CONV_DIM)), const((1, LANE)),
                const((1, LANE)), const((1, SSM_INNER)), const((1, SSM_INNER)),
                const((LANE, SSM_INNER))]
    assert len(in_specs) == N_SSD_IN
    operands = (z3, x3, d3, cbuf, h0, conv_w, conv_b.reshape(1, -1), lane_pad(dt_bias), lane_pad(a_log),
                jnp.repeat(d_skip, SSM_HEAD_DIM).reshape(1, -1), norm_g.reshape(1, -1), spread)
    out_specs = [chunk(SSM_INNER), state_spec]
    out_shape = [jax.ShapeDtypeStruct((batch, seq, SSM_INNER), BF16), jax.ShapeDtypeStruct(h0.shape, F32)]
    scratch = [pltpu.VMEM((SSM_HEADS, SSM_HEAD_DIM, SSM_STATE), F32), pltpu.VMEM((SUBLANE, SSM_CONV_DIM), F32)]
    group = SSD_SEQ_GROUP if (decode is None and nc == 1 and batch % SSD_SEQ_GROUP == 0) else 1
    if group > 1:
        seqs = lambda *tail: pl.BlockSpec((group,) + tail, lambda b, c: (b,) + (0,) * len(tail))
        g_in = [seqs(q, SSM_INNER), seqs(q, SSM_CONV_DIM), seqs(q, LANE), seqs(SUBLANE, SSM_CONV_DIM),
                seqs(SSM_HEADS, SSM_HEAD_DIM, SSM_STATE)] + in_specs[5:]
        y, hf = pl.pallas_call(
            functools.partial(_ssd_group_kernel, valid=valid, group=group),
            grid=(batch // group, 1), in_specs=g_in,
            out_specs=[seqs(q, SSM_INNER), seqs(SSM_HEADS, SSM_HEAD_DIM, SSM_STATE)], out_shape=out_shape,
            scratch_shapes=[pltpu.VMEM((group, SSM_HEADS, SSM_HEAD_DIM, SSM_STATE), F32),
                            pltpu.VMEM((group, SUBLANE, SSM_CONV_DIM), F32)],
            compiler_params=_cparams("parallel", "arbitrary"), name="ssd_scan_group",
        )(*operands)
        return y.reshape(m, SSM_INNER), hf
    if decode is None:
        y, hf = pl.pallas_call(
            functools.partial(_ssd_kernel, valid=valid),
            grid=(batch, nc), in_specs=in_specs, out_specs=out_specs, out_shape=out_shape,
            scratch_shapes=scratch, compiler_params=_cparams("parallel", "arbitrary"), name="ssd_scan",
        )(*operands)
        return y.reshape(m, SSM_INNER), hf

    b_dec, n_q, hw = decode["qb"].shape
    n_pages = decode["page_table"].shape[1]
    steps_per_seq, pages = _decode_split(batch * nc, b_dec, n_pages)
    pt, qrows, kn, vn, ck, cv = _decode_operands(decode["qb"], decode["kb_new"], decode["vb_new"],
                                                 decode["cache_k"], decode["cache_v"],
                                                 decode["page_table"], decode["layer"])
    n_rows = qrows.shape[1]
    step_of = lambda b, c: b * nc + c

    def page_spec(i):
        return pl.BlockSpec((None,) + ck.shape[1:], lambda b, c, pt_ref: (pt_ref[step_of(b, c) * pages + i], 0, 0))

    per_seq = lambda r, w: pl.BlockSpec((None, r, w), lambda b, c, pt_ref: (step_of(b, c) // steps_per_seq, 0, 0))
    dec_specs = [const((4, ATTN_DK)), const((1, ATTN_DV)), per_seq(n_rows, ATTN_HD), per_seq(NEW_ROWS, hw),
                 per_seq(NEW_ROWS, hw)] + [page_spec(i) for i in range(pages)] * 2
    grid_spec = pltpu.PrefetchScalarGridSpec(
        num_scalar_prefetch=1, grid=(batch, nc),
        in_specs=in_specs + dec_specs,
        out_specs=out_specs + [per_seq(n_q, hw)],
        scratch_shapes=scratch + _decode_scratch(n_q),
    )
    y, hf, o_dec = pl.pallas_call(
        functools.partial(_ssd_decode_kernel, valid=valid, pages=pages, n_q=n_q,
                          lam_init=decode["lam_init"], steps_per_seq=steps_per_seq),
        grid_spec=grid_spec,
        out_shape=out_shape + [jax.ShapeDtypeStruct((b_dec, n_q, hw), BF16)],
        compiler_params=_cparams("arbitrary", "arbitrary"),
        name="ssd_scan_decode",
    )(pt, *operands, decode["lam_p"], decode["subln_g"].reshape(1, ATTN_DV), qrows, kn, vn,
      *([ck] * pages), *([cv] * pages))
    return y.reshape(m, SSM_INNER), hf, o_dec


def _ssm_group(h, batch, seq, conv_buf, h0, g_mix, w_in_all, layer, w_dt, conv_w, conv_b, dt_bias, a_log,
               d_skip, norm_g, decode=None):
    d = h.shape[1]
    if seq % SSM_CHUNK == 0:
        padded = seq
        chunk_len = valid = SSM_CHUNK
        hin = h
    else:
        assert seq <= SSM_SHORT_CHUNK
        padded = chunk_len = SSM_SHORT_CHUNK
        valid = seq
        hin = jnp.pad(h.reshape(batch, seq, d), ((0, 0), (0, padded - seq), (0, 0))).reshape(-1, d)
    z, xbc, dt = ssm_in_proj(hin, g_mix, w_in_all, layer, w_dt)
    y, hf, *dec_out = ssd_scan(z, xbc, dt, conv_buf, h0, conv_w, conv_b, dt_bias, a_log, d_skip, norm_g,
                               batch=batch, chunk_len=chunk_len, valid=valid, decode=decode)
    xbc3 = xbc.reshape(batch, padded, SSM_CONV_DIM)[:, :seq]
    new_buf = jnp.concatenate([conv_buf, xbc3], axis=1)[:, seq:]
    y = y.reshape(batch, padded, SSM_INNER)[:, :seq].reshape(batch * seq, SSM_INNER)
    return (y, new_buf, hf, *dec_out)


def kernel(x_prompt, x_sample, cache_k, cache_v, state_ssm, state_conv, page_table, p_prompt, p_sample, norm_mix_g, norm_mlp_g, norm_ple_g, mlp_w_up, mlp_w_down, ple_w_gate, ple_w_proj, final_norm_g, sgu_w_in, sgu_ln_g, sgu_ln_b, sgu_w_s, sgu_b_s, sgu_w_out, attn_w_qkv, attn_lambda, attn_subln_g, attn_w_out, ssm_w_in, ssm_conv_w, ssm_conv_b, ssm_dt_bias, ssm_a_log, ssm_d, ssm_norm_g, ssm_w_out):
    b_p, l_p, d = x_prompt.shape
    b_s, l_s, _ = x_sample.shape
    depth = norm_mix_g.shape[0]
    past_len = page_table.shape[1] * cache_k.shape[2]
    m_s = b_s * l_s
    assert m_s % SGU_CHUNK == 0 and SGU_CHUNK % l_s == 0 and l_p % SGU_CHUNK == 0
    bf = lambda w: w.astype(BF16)

    hp = x_prompt.reshape(b_p * l_p, d)
    hs = x_sample.reshape(m_s, d)
    rope_p = _rope_tables(jnp.arange(l_p, dtype=jnp.int32))
    rope_s = _rope_tables(past_len + jnp.arange(l_s, dtype=jnp.int32))

    w_up, w_down, w_gate, w_proj = bf(mlp_w_up), bf(mlp_w_down), bf(ple_w_gate), bf(ple_w_proj)
    w_sgu_in, w_sgu_out = bf(sgu_w_in), bf(sgu_w_out)
    w_qkv, w_attn_out = bf(attn_w_qkv), bf(attn_w_out)
    w_ssm_in, w_ssm_out = bf(ssm_w_in), bf(ssm_w_out)
    pp = p_prompt.reshape(depth, b_p * l_p, D_PLE)
    ps = p_sample.reshape(depth, m_s, D_PLE)

    k_p, v_p, k_s, v_s = [], [], [], []
    ssm_p, conv_p, ssm_s, conv_s = [], [], [], []
    sgu_s = []

    def channel(h, mix, p_all, i, w_mix, j, tile):
        return channel_update(h, mix, p_all, i, w_mix, j, norm_mlp_g[i], w_up, w_down, norm_ple_g[i],
                              w_gate, w_proj, final_norm_g, final_norm=i == depth - 1, tile=tile)

    def decode_rides_along(i):
        if i + 1 >= depth or (i + 1) % N_MIXERS != 2 or l_p % SSM_CHUNK:
            return False
        return _decode_split(b_p * (l_p // SSM_CHUNK), b_s, page_table.shape[1]) is not None

    pending = None
    for i in range(depth):
        kind, j = i % N_MIXERS, i // N_MIXERS
        g_mix = norm_mix_g[i]
        if kind == 0:
            ws_p, bs_p = _sgu_spatial(sgu_w_s[j], sgu_b_s[j], l_p)
            ws_s, bs_s = _sgu_spatial(sgu_w_s[j], sgu_b_s[j], l_s)
            mix_p, _ = sgu_mix(hp, g_mix, w_sgu_in, j, sgu_ln_g[j], sgu_ln_b[j], ws_p, bs_p, want_v=False)
            mix_s, v_rows = sgu_mix(hs, g_mix, w_sgu_in, j, sgu_ln_g[j], sgu_ln_b[j], ws_s, bs_s,
                                    want_v=True, tile=SMALL_TILE)
            sgu_s.append(v_rows.reshape(b_s, l_s, SGU_WIDTH))
            w_mix = w_sgu_out
        elif kind == 1:
            lam_init = 0.8 - 0.6 * math.exp(-0.3 * i)
            qb, k, v, kb, vb = qkv_rope(hp, g_mix, w_qkv, j, *rope_p, transposed_v=True)
            mix_p = attn_prompt(qb, kb, vb, attn_lambda[j], attn_subln_g[j], batch=b_p, lam_init=lam_init)
            k_p.append(k.reshape(b_p, l_p, ATTN_HEADS, ATTN_HD))
            v_p.append(v.reshape(b_p, l_p, ATTN_HEADS, ATTN_DV))
            qb, k, v, kb, vb = qkv_rope(hs, g_mix, w_qkv, j, *rope_s, tile=SMALL_TILE)
            k_s.append(k.reshape(b_s, l_s, ATTN_HEADS, ATTN_HD))
            v_s.append(v.reshape(b_s, l_s, ATTN_HEADS, ATTN_DV))
            seq3 = lambda x: x.reshape(b_s, l_s, -1)
            dec = dict(qb=seq3(qb), kb_new=seq3(kb), vb_new=seq3(vb), cache_k=cache_k, cache_v=cache_v,
                       page_table=page_table, layer=j, lam_p=attn_lambda[j], subln_g=attn_subln_g[j],
                       lam_init=lam_init)
            w_mix = w_attn_out
            if decode_rides_along(i):
                pending = (dec, (ps, i, w_mix, j))
                mix_s = None
            else:
                mix_s = attn_decode(**dec).reshape(m_s, -1)
        else:
            w_dt = jnp.pad(w_ssm_in[j][:, SSM_INNER + SSM_CONV_DIM:], ((0, 0), (0, LANE - SSM_HEADS)))
            shared = (g_mix, w_ssm_in, j, w_dt, ssm_conv_w[j], ssm_conv_b[j], ssm_dt_bias[j],
                      ssm_a_log[j], ssm_d[j], ssm_norm_g[j])
            zero_buf = jnp.zeros((b_p, SSM_CONV - 1, SSM_CONV_DIM), F32)
            zero_h = jnp.zeros((b_p, SSM_HEADS, SSM_HEAD_DIM, SSM_STATE), F32)
            if pending is None:
                mix_p, cb_p, hf_p = _ssm_group(hp, b_p, l_p, zero_buf, zero_h, *shared)
            else:
                dec, prev_channel = pending
                pending = None
                mix_p, cb_p, hf_p, o_dec = _ssm_group(hp, b_p, l_p, zero_buf, zero_h, *shared, decode=dec)
                hs = channel(hs, o_dec.reshape(m_s, -1), *prev_channel, SMALL_TILE)
            mix_s, cb_s, hf_s = _ssm_group(hs, b_s, l_s, state_conv[j], state_ssm[j], *shared)
            ssm_p.append(hf_p)
            conv_p.append(cb_p)
            ssm_s.append(hf_s)
            conv_s.append(cb_s)
            w_mix = w_ssm_out
        hp = channel(hp, mix_p, pp, i, w_mix, j, ROW_TILE)
        if mix_s is not None:
            hs = channel(hs, mix_s, ps, i, w_mix, j, SMALL_TILE)

    return (hp.reshape(b_p, l_p, d), hs.reshape(b_s, l_s, d),
            jnp.stack(k_p), jnp.stack(v_p), jnp.stack(k_s), jnp.stack(v_s),
            jnp.stack(ssm_p), jnp.stack(conv_p), jnp.stack(ssm_s), jnp.stack(conv_s),
            jnp.stack(sgu_s))
```
